```python
import jax, jax.numpy as jnp
from jax import lax
import numpy as np

D_MODEL = 4096
BATCH = 4
SEQ = 2048
DEPTH = 2
DEC_BATCH = 128
DEC_SEQ = 8
PAST_LEN = 16384
PAGE_SIZE = 128

N_A_LAYERS = DEPTH // 2
N_B_LAYERS = DEPTH - N_A_LAYERS
N_DENSE_LAYERS = (DEPTH + 1) // 2
N_MOE_LAYERS = DEPTH // 2
ALPHA = (2.0 * DEPTH) ** 0.25
BETA = (8.0 * DEPTH) ** -0.25
NORM_EPS = 1e-5
D_CONV = D_MODEL
CONV_W = 3
N_HEADS = 32
QK_NOPE = 128
QK_ROPE = 64
V_HEAD = 128
Q_LORA = 1024
KV_LORA = 512
KV_DIM = KV_LORA + QK_ROPE
ROPE_THETA = 10000.0
SOFTMAX_SCALE = (QK_NOPE + QK_ROPE) ** -0.5
Q_BLOCK = 128
D_FF = 14336
N_EXPERTS = 8
TOP_K = 2
D_EXPERT = 14336

kernel_name = "yoco_shortconv_mla_moe_step"

F32 = jnp.float32


def layer_norm(x, g, b):
    xf = x.astype(F32)
    mu = jnp.mean(xf, -1, keepdims=True)
    var = jnp.mean(jnp.square(xf - mu), -1, keepdims=True)
    return ((xf - mu) * lax.rsqrt(var + NORM_EPS) * g.astype(F32) + b.astype(F32)).astype(x.dtype)


def rms_norm(x, g):
    xf = x.astype(F32)
    return (xf * lax.rsqrt(jnp.mean(xf * xf, -1, keepdims=True) + NORM_EPS) * g.astype(F32)).astype(x.dtype)


def rope_tables(pos):
    inv = ROPE_THETA ** (-jnp.arange(0, QK_ROPE, 2, dtype=F32) / QK_ROPE)
    ang = pos.astype(F32)[:, None] * inv[None, :]
    return jnp.cos(ang), jnp.sin(ang)


def apply_rope(x, cos, sin):
    x1, x2 = jnp.split(x.astype(F32), 2, axis=-1)
    return jnp.concatenate([x1 * cos - x2 * sin, x2 * cos + x1 * sin], -1).astype(x.dtype)


def short_conv(x, prev, w_in, w_conv, w_out):
    s = x.shape[1]
    gb, gc, h = jnp.split(x @ w_in, 3, axis=-1)
    u = gc * h
    u_ext = jnp.concatenate([prev.astype(u.dtype), u], axis=1)
    v = w_conv[0] * u_ext[:, 0:s]
    for k in range(1, CONV_W):
        v = v + w_conv[k] * u_ext[:, k:k + s]
    return (gb * v) @ w_out, u_ext[:, -(CONV_W - 1):]


def shared_kv(x, cos, sin, kv_w_down, kv_norm_g):
    ckr = x @ kv_w_down
    c = rms_norm(ckr[..., :KV_LORA], kv_norm_g)
    kr = apply_rope(ckr[..., KV_LORA:], cos, sin)
    return jnp.concatenate([c, kr], axis=-1)


def mla(x, kv_rows, cos, sin, attend, q_w_down, q_norm_g, q_w_up, w_uk, w_uv, w_out):
    b, s, _ = x.shape
    cq = rms_norm(x @ q_w_down, q_norm_g)
    q = jnp.einsum('bsc,chd->bshd', cq, q_w_up)
    q_rope = apply_rope(q[..., QK_NOPE:], cos[:, None, :], sin[:, None, :])
    q_lat = jnp.einsum('bshd,chd->bshc', q[..., :QK_NOPE], w_uk)
    q_cat = jnp.concatenate([q_lat, q_rope], axis=-1) * SOFTMAX_SCALE
    o_lat = attend(q_cat, kv_rows)
    o = jnp.einsum('bshc,chd->bshd', o_lat, w_uv).reshape(b, s, N_HEADS * V_HEAD)
    return o @ w_out


def attend_prompt(q_cat, kv):
    b, s, h, e = q_cat.shape
    nb = s // Q_BLOCK
    qb = q_cat.reshape(b, nb, Q_BLOCK, h, e).transpose(1, 0, 2, 3, 4)
    key_pos = jnp.arange(s)
    values = kv[..., :KV_LORA]

    def block(args):
        q_blk, start = args
        sc = jnp.einsum('bqhe,bke->bhqk', q_blk, kv).astype(F32)
        q_pos = start + jnp.arange(Q_BLOCK)
        sc = jnp.where(key_pos[None, :] <= q_pos[:, None], sc, -jnp.inf)
        pr = jax.nn.softmax(sc, axis=-1).astype(kv.dtype)
        return jnp.einsum('bhqk,bkc->bqhc', pr, values)

    out = lax.map(block, (qb, jnp.arange(nb) * Q_BLOCK))
    return out.transpose(1, 0, 2, 3, 4).reshape(b, s, h, KV_LORA)


def attend_paged(q_cat, kv_new, cache, page_table):
    n_new = q_cat.shape[1]
    past_len = page_table.shape[1] * PAGE_SIZE
    key_pos = jnp.arange(past_len + n_new)
    q_pos = past_len + jnp.arange(n_new)
    mask = key_pos[None, :] <= q_pos[:, None]

    def one_seq(args):
        q, new_rows, pages = args
        past = cache[pages].reshape(past_len, KV_DIM).astype(new_rows.dtype)
        keys = jnp.concatenate([past, new_rows], axis=0)
        sc = jnp.einsum('qhe,ke->hqk', q, keys).astype(F32)
        sc = jnp.where(mask[None], sc, -jnp.inf)
        pr = jax.nn.softmax(sc, axis=-1).astype(keys.dtype)
        return jnp.einsum('hqk,kc->qhc', pr, keys[:, :KV_LORA])

    return lax.map(one_seq, (q_cat, kv_new, page_table))


def swiglu(x, w_gate, w_up, w_down):
    return (jax.nn.silu(x @ w_gate) * (x @ w_up)) @ w_down


def moe(x, router_w, w_gate, w_up, w_down):
    logits = (x @ router_w).astype(F32)
    top_val, top_idx = lax.top_k(logits, TOP_K)
    gates = jax.nn.softmax(top_val, axis=-1)
    dense_gate = jnp.sum(jax.nn.one_hot(top_idx, N_EXPERTS, dtype=F32) * gates[..., None], axis=-2).astype(x.dtype)
    y = dense_gate[..., 0:1] * swiglu(x, w_gate[0], w_up[0], w_down[0])
    for e in range(1, N_EXPERTS):
        y = y + dense_gate[..., e:e + 1] * swiglu(x, w_gate[e], w_up[e], w_down[e])
    return y


def trunk(x, conv_prev, cos, sin, attend, p):
    new_conv = []
    kv_rows = None
    for layer in range(DEPTH):
        if layer < N_A_LAYERS:
            h, st = short_conv(x, conv_prev[layer], p['conv_w_in'][layer], p['conv_w'][layer], p['conv_w_out'][layer])
            new_conv.append(st)
        else:
            if layer == N_A_LAYERS:
                kv_rows = shared_kv(x, cos, sin, p['kv_w_down'], p['kv_norm_g'])
            j = layer - N_A_LAYERS
            h = mla(x, kv_rows, cos, sin, attend, p['q_w_down'][j], p['q_norm_g'][j], p['q_w_up'][j],
                    p['w_uk'], p['w_uv'], p['attn_w_out'][j])
        x = layer_norm(ALPHA * x + h, p['ln_mix_g'][layer], p['ln_mix_b'][layer])
        i = layer // 2
        if layer % 2 == 0:
            f = swiglu(x, p['ffn_w_gate'][i], p['ffn_w_up'][i], p['ffn_w_down'][i])
        else:
            f = moe(x, p['router_w'][i], p['expert_w_gate'][i], p['expert_w_up'][i], p['expert_w_down'][i])
        x = layer_norm(ALPHA * x + f, p['ln_ffn_g'][layer], p['ln_ffn_b'][layer])
    return x, jnp.stack(new_conv, axis=0), kv_rows


def setup_inputs(seed: int = 0) -> dict:
    key = jax.random.key(seed)
    ks = jax.random.split(key, 32)

    def nrm(k, shape, scale):
        return jax.random.normal(k, shape, F32) * scale

    n_pages = PAST_LEN // PAGE_SIZE
    n_used = DEC_BATCH * n_pages
    n_pool = n_used + (n_used + 3) // 4
    page_table = jax.random.permutation(ks[4], n_pool)[:n_used].reshape(DEC_BATCH, n_pages).astype(jnp.int32)
    d = D_MODEL
    return {
        'x_prompt': nrm(ks[0], (BATCH, SEQ, d), 1.0),
        'x_sample': nrm(ks[1], (DEC_BATCH, DEC_SEQ, d), 1.0),
        'state_conv': nrm(ks[2], (N_A_LAYERS, DEC_BATCH, CONV_W - 1, D_CONV), 1.0),
        'cache_mla': nrm(ks[3], (n_pool, PAGE_SIZE, KV_DIM), 1.0),
        'page_table': page_table,
        'ln_mix_g': 1.0 + nrm(ks[5], (DEPTH, d), 0.05),
        'ln_mix_b': nrm(ks[6], (DEPTH, d), 0.02),
        'ln_ffn_g': 1.0 + nrm(ks[7], (DEPTH, d), 0.05),
        'ln_ffn_b': nrm(ks[8], (DEPTH, d), 0.02),
        'conv_w_in': nrm(ks[9], (N_A_LAYERS, d, 3 * D_CONV), d ** -0.5),
        'conv_w': nrm(ks[10], (N_A_LAYERS, CONV_W, D_CONV), CONV_W ** -0.5),
        'conv_w_out': nrm(ks[11], (N_A_LAYERS, D_CONV, d), BETA * D_CONV ** -0.5),
        'kv_w_down': nrm(ks[12], (d, KV_DIM), d ** -0.5),
        'kv_norm_g': 1.0 + nrm(ks[13], (KV_LORA,), 0.05),
        'w_uk': nrm(ks[14], (KV_LORA, N_HEADS, QK_NOPE), KV_LORA ** -0.5),
        'w_uv': nrm(ks[15], (KV_LORA, N_HEADS, V_HEAD), BETA * KV_LORA ** -0.5),
        'q_w_down': nrm(ks[16], (N_B_LAYERS, d, Q_LORA), d ** -0.5),
        'q_norm_g': 1.0 + nrm(ks[17], (N_B_LAYERS, Q_LORA), 0.05),
        'q_w_up': nrm(ks[18], (N_B_LAYERS, Q_LORA, N_HEADS, QK_NOPE + QK_ROPE), Q_LORA ** -0.5),
        'attn_w_out': nrm(ks[19], (N_B_LAYERS, N_HEADS * V_HEAD, d), BETA * (N_HEADS * V_HEAD) ** -0.5),
        'ffn_w_gate': nrm(ks[20], (N_DENSE_LAYERS, d, D_FF), d ** -0.5),
        'ffn_w_up': nrm(ks[21], (N_DENSE_LAYERS, d, D_FF), d ** -0.5),
        'ffn_w_down': nrm(ks[22], (N_DENSE_LAYERS, D_FF, d), BETA * D_FF ** -0.5),
        'router_w': nrm(ks[23], (N_MOE_LAYERS, d, N_EXPERTS), d ** -0.5),
        'expert_w_gate': nrm(ks[24], (N_MOE_LAYERS, N_EXPERTS, d, D_EXPERT), d ** -0.5),
        'expert_w_up': nrm(ks[25], (N_MOE_LAYERS, N_EXPERTS, d, D_EXPERT), d ** -0.5),
        'expert_w_down': nrm(ks[26], (N_MOE_LAYERS, N_EXPERTS, D_EXPERT, d), BETA * D_EXPERT ** -0.5),
    }


def reference(x_prompt, x_sample, state_conv, cache_mla, page_table,
              ln_mix_g, ln_mix_b, ln_ffn_g, ln_ffn_b,
              conv_w_in, conv_w, conv_w_out,
              kv_w_down, kv_norm_g, w_uk, w_uv,
              q_w_down, q_norm_g, q_w_up, attn_w_out,
              ffn_w_gate, ffn_w_up, ffn_w_down,
              router_w, expert_w_gate, expert_w_up, expert_w_down):
    p = dict(ln_mix_g=ln_mix_g, ln_mix_b=ln_mix_b, ln_ffn_g=ln_ffn_g, ln_ffn_b=ln_ffn_b,
             conv_w_in=conv_w_in, conv_w=conv_w, conv_w_out=conv_w_out,
             kv_w_down=kv_w_down, kv_norm_g=kv_norm_g, w_uk=w_uk, w_uv=w_uv,
             q_w_down=q_w_down, q_norm_g=q_norm_g, q_w_up=q_w_up, attn_w_out=attn_w_out,
             ffn_w_gate=ffn_w_gate, ffn_w_up=ffn_w_up, ffn_w_down=ffn_w_down,
             router_w=router_w, expert_w_gate=expert_w_gate, expert_w_up=expert_w_up,
             expert_w_down=expert_w_down)
    b, s, _ = x_prompt.shape
    cos_p, sin_p = rope_tables(jnp.arange(s))
    conv_zero = jnp.zeros((N_A_LAYERS, b, CONV_W - 1, D_CONV), x_prompt.dtype)
    y_prompt, conv_prompt, rows_prompt = trunk(x_prompt, conv_zero, cos_p, sin_p, attend_prompt, p)
    past_len = page_table.shape[1] * PAGE_SIZE
    cos_s, sin_s = rope_tables(past_len + jnp.arange(x_sample.shape[1]))
    attend_s = lambda q, kv: attend_paged(q, kv, cache_mla, page_table)
    y_sample, conv_sample, rows_sample = trunk(x_sample, state_conv, cos_s, sin_s, attend_s, p)
    return (y_prompt, y_sample, conv_prompt, conv_sample, rows_prompt, rows_sample)
```

```python
import functools

import jax
import jax.numpy as jnp
from jax import lax
from jax.experimental import pallas as pl
from jax.experimental.pallas import tpu as pltpu

F32 = jnp.float32
BF16 = jnp.bfloat16

DEPTH = 2
ALPHA = (2.0 * DEPTH) ** 0.25
NORM_EPS = 1e-5
ROPE_THETA = 10000.0
TOP_K = 2
LANES = 128
VMEM_LIMIT_BYTES = 56 * 1024 * 1024
MOE_TILE = 512


def _params(*semantics):
    return pltpu.CompilerParams(dimension_semantics=semantics, vmem_limit_bytes=VMEM_LIMIT_BYTES)


def _blk(dim, pref):
    b = min(dim, pref)
    while dim % b:
        b //= 2
    return b


def _dot(a, b):
    return jnp.dot(a, b, preferred_element_type=F32)


def _rope(x, cos2, sin2):
    half = x.shape[-1] // 2
    rot = jnp.concatenate([x[:, half:], x[:, :half]], axis=-1)
    return x * cos2 + rot * sin2


def _mm_kernel(x_ref, w_ref, o_ref, *scratch, nk):
    part = _dot(x_ref[...], w_ref[...].astype(BF16))
    if nk == 1:
        o_ref[...] = part.astype(o_ref.dtype)
        return
    (acc_ref,) = scratch
    k = pl.program_id(2)

    @pl.when(k == 0)
    def _():
        acc_ref[...] = part

    @pl.when(k > 0)
    def _():
        acc_ref[...] += part

    @pl.when(k == nk - 1)
    def _():
        o_ref[...] = acc_ref[...].astype(o_ref.dtype)


def _matmul(x, w, *, bm, bn, bk, out_dtype):
    m, kdim = x.shape
    n = w.shape[1]
    bm, bn, bk = _blk(m, bm), _blk(n, bn), _blk(kdim, bk)
    nk = kdim // bk
    return pl.pallas_call(
        functools.partial(_mm_kernel, nk=nk),
        grid=(m // bm, n // bn, nk),
        in_specs=[pl.BlockSpec((bm, bk), lambda i, j, k: (i, k)),
                  pl.BlockSpec((bk, bn), lambda i, j, k: (k, j))],
        out_specs=pl.BlockSpec((bm, bn), lambda i, j, k: (i, j)),
        out_shape=jax.ShapeDtypeStruct((m, n), out_dtype),
        scratch_shapes=[pltpu.VMEM((bm, bn), F32)] if nk > 1 else [],
        compiler_params=_params("arbitrary", "arbitrary", "arbitrary"),
        name="matmul",
    )(x, w)


def _conv_in_kernel(x_ref, wb_ref, wc_ref, wh_ref, gb_ref, u_ref):
    x = x_ref[...]
    gb_ref[...] = _dot(x, wb_ref[...])
    u_ref[...] = _dot(x, wc_ref[...]) * _dot(x, wh_ref[...])


def _conv_in(x, w_in):
    m, d = x.shape
    dc = w_in.shape[1] // 3
    bm, bn = _blk(m, 1024), _blk(dc, 256)
    nb = dc // bn
    w_spec = lambda part: pl.BlockSpec((d, bn), lambda i, j: (0, j + part * nb))
    return pl.pallas_call(
        _conv_in_kernel,
        grid=(m // bm, nb),
        in_specs=[pl.BlockSpec((bm, d), lambda i, j: (i, 0)), w_spec(0), w_spec(1), w_spec(2)],
        out_specs=[pl.BlockSpec((bm, bn), lambda i, j: (i, j))] * 2,
        out_shape=[jax.ShapeDtypeStruct((m, dc), F32)] * 2,
        compiler_params=_params("arbitrary", "arbitrary"),
        name="conv_in",
    )(x, w_in, w_in, w_in)


def _conv_gate_kernel(*refs, seq_len, has_hist):
    if has_hist:
        u_ref, gb_ref, w_ref, h1_ref, h2_ref, g_ref = refs
    else:
        u_ref, gb_ref, w_ref, g_ref = refs
    u = u_ref[...]
    t = lax.broadcasted_iota(jnp.int32, u.shape, 0) % seq_len
    prev1 = h1_ref[...] if has_hist else 0.0
    prev2 = h2_ref[...] if has_hist else 0.0
    s1 = jnp.where(t >= 1, pltpu.roll(u, 1, axis=0), prev1)
    s2 = jnp.where(t >= 2, pltpu.roll(u, 2, axis=0), prev2)
    w = w_ref[...]
    v = w[0:1, :] * s2 + w[1:2, :] * s1 + w[2:3, :] * u
    g_ref[...] = (gb_ref[...] * v).astype(g_ref.dtype)


def _conv_gate(u, gb, w_conv, hist, *, seq_len, rows):
    m, dc = u.shape
    bc = _blk(dc, 256)
    blk = pl.BlockSpec((rows, bc), lambda i, j: (i, j))
    ins = [u, gb, w_conv] + (list(hist) if hist is not None else [])
    in_specs = [blk, blk, pl.BlockSpec((w_conv.shape[0], bc), lambda i, j: (0, j))]
    in_specs += [blk, blk] if hist is not None else []
    return pl.pallas_call(
        functools.partial(_conv_gate_kernel, seq_len=seq_len, has_hist=hist is not None),
        grid=(m // rows, dc // bc),
        in_specs=in_specs,
        out_specs=blk,
        out_shape=jax.ShapeDtypeStruct((m, dc), BF16),
        compiler_params=_params("arbitrary", "arbitrary"),
        name="conv_gate",
    )(*ins)


def _ln_kernel(x_ref, h_ref, g_ref, b_ref, y_ref, yb_ref):
    z = ALPHA * x_ref[...] + h_ref[...]
    mu = jnp.mean(z, axis=-1, keepdims=True)
    zc = z - mu
    var = jnp.mean(zc * zc, axis=-1, keepdims=True)
    y = zc * lax.rsqrt(var + NORM_EPS) * g_ref[...] + b_ref[...]
    y_ref[...] = y
    yb_ref[...] = y.astype(BF16)


def _ln_residual(x, h, g, b):
    m, d = x.shape
    bm = _blk(m, 256)
    row = pl.BlockSpec((bm, d), lambda i: (i, 0))
    vec = pl.BlockSpec((1, d), lambda i: (0, 0))
    return pl.pallas_call(
        _ln_kernel,
        grid=(m // bm,),
        in_specs=[row, row, vec, vec],
        out_specs=[row, row],
        out_shape=[jax.ShapeDtypeStruct((m, d), F32), jax.ShapeDtypeStruct((m, d), BF16)],
        compiler_params=_params("arbitrary"),
        name="ln_residual",
    )(x, h, g.reshape(1, d), b.reshape(1, d))


def _glu_kernel(x_ref, wg_ref, wu_ref, o_ref):
    x = x_ref[...]
    g = _dot(x, wg_ref[...].astype(BF16))
    u = _dot(x, wu_ref[...].astype(BF16))
    o_ref[...] = (jax.nn.silu(g) * u).astype(o_ref.dtype)


def _glu(x, wg, wu):
    m, d = x.shape
    n = wg.shape[1]
    bm, bn = _blk(m, 1024), _blk(n, 512)
    w_spec = pl.BlockSpec((d, bn), lambda i, j: (0, j))
    return pl.pallas_call(
        _glu_kernel,
        grid=(m // bm, n // bn),
        in_specs=[pl.BlockSpec((bm, d), lambda i, j: (i, 0)), w_spec, w_spec],
        out_specs=pl.BlockSpec((bm, bn), lambda i, j: (i, j)),
        out_shape=jax.ShapeDtypeStruct((m, n), BF16),
        compiler_params=_params("arbitrary", "arbitrary"),
        name="glu",
    )(x, wg, wu)


def _kv_rows_kernel(x_ref, w_ref, g_ref, cos_ref, sin_ref, o_ref, ob_ref):
    kv_lora = g_ref.shape[-1]
    ckr = _dot(x_ref[...], w_ref[...])
    c = ckr[:, :kv_lora]
    c = c * lax.rsqrt(jnp.mean(c * c, axis=-1, keepdims=True) + NORM_EPS) * g_ref[...]
    kr = _rope(ckr[:, kv_lora:], cos_ref[...], sin_ref[...])
    out = jnp.concatenate([c, kr], axis=-1)
    o_ref[...] = out
    ob_ref[...] = out.astype(BF16)


def _kv_rows(x, w, g, cos2, sin2):
    m, d = x.shape
    e = w.shape[1]
    rope = cos2.shape[1]
    bm = _blk(m, 512)
    row = lambda width: pl.BlockSpec((bm, width), lambda i: (i, 0))
    return pl.pallas_call(
        _kv_rows_kernel,
        grid=(m // bm,),
        in_specs=[row(d), pl.BlockSpec((d, e), lambda i: (0, 0)),
                  pl.BlockSpec((1, e - rope), lambda i: (0, 0)), row(rope), row(rope)],
        out_specs=[row(e), row(e)],
        out_shape=[jax.ShapeDtypeStruct((m, e), F32), jax.ShapeDtypeStruct((m, e), BF16)],
        compiler_params=_params("arbitrary"),
        name="kv_rows",
    )(x, w, g.reshape(1, -1), cos2, sin2)


def _mm_rms_kernel(x_ref, w_ref, g_ref, o_ref):
    c = _dot(x_ref[...], w_ref[...])
    c = c * lax.rsqrt(jnp.mean(c * c, axis=-1, keepdims=True) + NORM_EPS) * g_ref[...]
    o_ref[...] = c.astype(o_ref.dtype)


def _mm_rms(x, w, g):
    m, d = x.shape
    n = w.shape[1]
    bm = _blk(m, 512)
    return pl.pallas_call(
        _mm_rms_kernel,
        grid=(m // bm,),
        in_specs=[pl.BlockSpec((bm, d), lambda i: (i, 0)), pl.BlockSpec((d, n), lambda i: (0, 0)),
                  pl.BlockSpec((1, n), lambda i: (0, 0))],
        out_specs=pl.BlockSpec((bm, n), lambda i: (i, 0)),
        out_shape=jax.ShapeDtypeStruct((m, n), BF16),
        compiler_params=_params("arbitrary"),
        name="q_down_rms",
    )(x, w, g.reshape(1, n))


def _q_heads_kernel(cq_ref, wn_ref, wr_ref, wuk_ref, cos_ref, sin_ref, o_ref, *, scale):
    cq = cq_ref[...]
    q_nope = _dot(cq, wn_ref[0])
    q_pe = _rope(_dot(cq, wr_ref[0]), cos_ref[...], sin_ref[...])
    q_lat = _dot(q_nope.astype(BF16), wuk_ref[0])
    o_ref[0] = (jnp.concatenate([q_lat, q_pe], axis=-1) * scale).astype(o_ref.dtype)


def _q_heads(cq, w_nope, w_rope, w_uk_t, cos2, sin2, *, row0, nrows, scale, out_dtype):
    ql = cq.shape[1]
    h, _, nope = w_nope.shape
    rope = w_rope.shape[2]
    lat = w_uk_t.shape[2]
    bm = _blk(nrows, 1024)
    assert row0 % bm == 0
    r0 = row0 // bm
    row = lambda width: pl.BlockSpec((bm, width), lambda i, j: (i + r0, 0))
    head = lambda a, b: pl.BlockSpec((1, a, b), lambda i, j: (j, 0, 0))
    return pl.pallas_call(
        functools.partial(_q_heads_kernel, scale=scale),
        grid=(nrows // bm, h),
        in_specs=[row(ql), head(ql, nope), head(ql, rope), head(nope, lat), row(rope), row(rope)],
        out_specs=pl.BlockSpec((1, bm, lat + rope), lambda i, j: (j, i, 0)),
        out_shape=jax.ShapeDtypeStruct((h, nrows, lat + rope), out_dtype),
        compiler_params=_params("arbitrary", "arbitrary"),
        name="q_heads",
    )(cq, w_nope, w_rope, w_uk_t, cos2, sin2)


def _softmax_step(s, v, m_ref, l_ref, acc_ref):
    m_prev = m_ref[...]
    m_new = jnp.maximum(m_prev, jnp.max(s, axis=-1, keepdims=True))
    alpha = jnp.exp(m_prev - m_new)
    p = jnp.exp(s - m_new[:, 0:1])
    l_ref[...] = alpha * l_ref[...] + jnp.sum(p, axis=-1, keepdims=True)
    acc_ref[...] = acc_ref[...] * alpha[:, 0:1] + _dot(p.astype(BF16), v)
    m_ref[...] = m_new


def _softmax_init(m_ref, l_ref, acc_ref):
    m_ref[...] = jnp.full(m_ref.shape, -jnp.inf, F32)
    l_ref[...] = jnp.zeros(l_ref.shape, F32)
    acc_ref[...] = jnp.zeros(acc_ref.shape, F32)


def _scores(q, k):
    return lax.dot_general(q, k, (((1,), (1,)), ((), ())), preferred_element_type=F32)


def _attn_prompt_kernel(q_ref, kv_ref, o_ref, m_ref, l_ref, acc_ref, *, blk):
    qi = pl.program_id(1)
    hg, _, e = q_ref.shape
    lat = o_ref.shape[-1]
    rows = hg * blk
    q = q_ref[...].reshape(rows, e)
    _softmax_init(m_ref, l_ref, acc_ref)
    q_pos = qi * blk + lax.broadcasted_iota(jnp.int32, (rows, blk), 0) % blk
    k_off = lax.broadcasted_iota(jnp.int32, (rows, blk), 1)

    def body(j, carry):
        k = kv_ref[pl.ds(pl.multiple_of(j * blk, blk), blk), :]
        s = jnp.where(j * blk + k_off <= q_pos, _scores(q, k), -jnp.inf)
        _softmax_step(s, k[:, :lat], m_ref, l_ref, acc_ref)
        return carry

    lax.fori_loop(0, qi + 1, body, 0)
    out = acc_ref[...] / l_ref[...][:, 0:1]
    o_ref[...] = out.reshape(hg, blk, lat).astype(o_ref.dtype)


def _attn_prompt(q, kv, *, batch, seq, lat):
    h, _, e = q.shape
    blk = _blk(seq, 256)
    hg = _blk(h, 4)
    nq = seq // blk
    return pl.pallas_call(
        functools.partial(_attn_prompt_kernel, blk=blk),
        grid=(batch, nq, h // hg),
        in_specs=[pl.BlockSpec((hg, blk, e), lambda b, i, g: (g, b * nq + i, 0)),
                  pl.BlockSpec((seq, e), lambda b, i, g: (b, 0))],
        out_specs=pl.BlockSpec((hg, blk, lat), lambda b, i, g: (g, b * nq + i, 0)),
        out_shape=jax.ShapeDtypeStruct((h, batch * seq, lat), BF16),
        scratch_shapes=[pltpu.VMEM((hg * blk, LANES), F32), pltpu.VMEM((hg * blk, LANES), F32),
                        pltpu.VMEM((hg * blk, lat), F32)],
        compiler_params=_params("arbitrary", "arbitrary", "arbitrary"),
        name="attn_prompt",
    )(q, kv)


def _attn_paged_kernel(pt_ref, q_ref, new_ref, *rest, n_par, n_new):
    del pt_ref
    cache_refs = rest[:n_par]
    o_ref, m_ref, l_ref, acc_ref = rest[n_par:]
    g = pl.program_id(1)
    h, _, e = q_ref.shape
    lat = o_ref.shape[-1]
    rows = h * n_new
    q = q_ref[...].reshape(rows, e).astype(BF16)

    @pl.when(g == 0)
    def _():
        _softmax_init(m_ref, l_ref, acc_ref)

    k = jnp.concatenate([c[0].astype(BF16) for c in cache_refs], axis=0)
    _softmax_step(_scores(q, k), k[:, :lat], m_ref, l_ref, acc_ref)

    @pl.when(g == pl.num_programs(1) - 1)
    def _():
        kn = new_ref[0]
        q_pos = lax.broadcasted_iota(jnp.int32, (rows, kn.shape[0]), 0) % n_new
        k_pos = lax.broadcasted_iota(jnp.int32, (rows, kn.shape[0]), 1)
        s = jnp.where(k_pos <= q_pos, _scores(q, kn), -jnp.inf)
        _softmax_step(s, kn[:, :lat], m_ref, l_ref, acc_ref)
        out = acc_ref[...] / l_ref[...][:, 0:1]
        o_ref[...] = out.reshape(h, n_new, lat)


def _attn_paged(q, kv_new_pad, cache, page_table, *, lat):
    h, t, e = q.shape
    b, n_pages = page_table.shape
    n_new = t // b
    page = cache.shape[1]
    n_par = _blk(n_pages, 8)

    def cache_spec(p):
        return pl.BlockSpec((1, page, e), lambda i, g, pt: (pt[i * n_pages + g * n_par + p], 0, 0))

    grid_spec = pltpu.PrefetchScalarGridSpec(
        num_scalar_prefetch=1,
        grid=(b, n_pages // n_par),
        in_specs=[pl.BlockSpec((h, n_new, e), lambda i, g, pt: (0, i, 0)),
                  pl.BlockSpec((1, kv_new_pad.shape[1], e), lambda i, g, pt: (i, 0, 0))]
                 + [cache_spec(p) for p in range(n_par)],
        out_specs=pl.BlockSpec((h, n_new, lat), lambda i, g, pt: (0, i, 0)),
        scratch_shapes=[pltpu.VMEM((h * n_new, LANES), F32), pltpu.VMEM((h * n_new, LANES), F32),
                        pltpu.VMEM((h * n_new, lat), F32)],
    )
    return pl.pallas_call(
        functools.partial(_attn_paged_kernel, n_par=n_par, n_new=n_new),
        grid_spec=grid_spec,
        out_shape=jax.ShapeDtypeStruct((h, t, lat), F32),
        compiler_params=_params("arbitrary", "arbitrary"),
        name="attn_paged",
    )(page_table.reshape(-1), q, kv_new_pad, *([cache] * n_par))


def _o_heads_kernel(ol_ref, w_ref, o_ref):
    o_ref[...] = _dot(ol_ref[0].astype(BF16), w_ref[0]).astype(o_ref.dtype)


def _o_heads(o_lat, w_uv_t):
    h, m, lat = o_lat.shape
    v = w_uv_t.shape[2]
    bm = _blk(m, 1024)
    return pl.pallas_call(
        _o_heads_kernel,
        grid=(m // bm, h),
        in_specs=[pl.BlockSpec((1, bm, lat), lambda i, j: (j, i, 0)),
                  pl.BlockSpec((1, lat, v), lambda i, j: (j, 0, 0))],
        out_specs=pl.BlockSpec((bm, v), lambda i, j: (i, j)),
        out_shape=jax.ShapeDtypeStruct((m, h * v), BF16),
        compiler_params=_params("arbitrary", "arbitrary"),
        name="o_heads",
    )(o_lat, w_uv_t)


def _router_kernel(x_ref, w_ref, o_ref, *, n_experts):
    logits = jnp.dot(x_ref[...], w_ref[...], precision=lax.Precision.HIGHEST, preferred_element_type=F32)
    lane = lax.broadcasted_iota(jnp.int32, logits.shape, 1)
    logits = jnp.where(lane < n_experts, logits, -jnp.inf)
    v1 = jnp.max(logits, axis=-1, keepdims=True)
    i1 = jnp.min(jnp.where(logits == v1, lane, LANES), axis=-1, keepdims=True)
    rest = jnp.where(lane == i1, -jnp.inf, logits)
    v2 = jnp.max(rest, axis=-1, keepdims=True)
    i2 = jnp.min(jnp.where(rest == v2, lane, LANES), axis=-1, keepdims=True)
    e2 = jnp.exp(v2 - v1)
    denom = 1.0 + e2
    out = jnp.where(lane == 0, i1.astype(F32),
                    jnp.where(lane == 1, i2.astype(F32),
                              jnp.where(lane == 2, 1.0 / denom, e2 / denom)))
    o_ref[...] = out


def _router(x, router_w):
    m, d = x.shape
    n_experts = router_w.shape[1]
    w = jnp.pad(router_w, ((0, 0), (0, LANES - n_experts)))
    bm = _blk(m, 512)
    meta = pl.pallas_call(
        functools.partial(_router_kernel, n_experts=n_experts),
        grid=(m // bm,),
        in_specs=[pl.BlockSpec((bm, d), lambda i: (i, 0)), pl.BlockSpec((d, LANES), lambda i: (0, 0))],
        out_specs=pl.BlockSpec((bm, LANES), lambda i: (i, 0)),
        out_shape=jax.ShapeDtypeStruct((m, LANES), F32),
        compiler_params=_params("arbitrary"),
        name="router",
    )(x, w)
    return meta[:, :TOP_K].astype(jnp.int32), meta[:, TOP_K:2 * TOP_K]


def _moe_glu_kernel(te_ref, nu_ref, x_ref, wg_ref, wu_ref, o_ref):
    del te_ref
    t = pl.program_id(1)

    @pl.when(t < nu_ref[0])
    def _():
        x = x_ref[...]
        g = _dot(x, wg_ref[0].astype(BF16))
        u = _dot(x, wu_ref[0].astype(BF16))
        o_ref[...] = (jax.nn.silu(g) * u).astype(o_ref.dtype)

    @pl.when(t >= nu_ref[0])
    def _():
        o_ref[...] = jnp.zeros(o_ref.shape, o_ref.dtype)


def _moe_glu(x_sorted, w_gate, w_up, tile_expert, n_used):
    p, d = x_sorted.shape
    n = w_gate.shape[2]
    bm, bn = MOE_TILE, _blk(n, 256)
    last = lambda t, nu: jnp.minimum(t, nu[0] - 1)
    w_spec = pl.BlockSpec((1, d, bn), lambda j, t, te, nu: (te[t], 0, j))
    grid_spec = pltpu.PrefetchScalarGridSpec(
        num_scalar_prefetch=2,
        grid=(n // bn, p // bm),
        in_specs=[pl.BlockSpec((bm, d), lambda j, t, te, nu: (last(t, nu), 0)), w_spec, w_spec],
        out_specs=pl.BlockSpec((bm, bn), lambda j, t, te, nu: (t, j)),
    )
    return pl.pallas_call(
        _moe_glu_kernel,
        grid_spec=grid_spec,
        out_shape=jax.ShapeDtypeStruct((p, n), BF16),
        compiler_params=_params("arbitrary", "arbitrary"),
        name="moe_glu",
    )(tile_expert, n_used, x_sorted, w_gate, w_up)


def _moe_down_kernel(te_ref, nu_ref, h_ref, w_ref, gate_ref, o_ref, acc_ref, *, nk):
    del te_ref
    t = pl.program_id(0)
    k = pl.program_id(2)
    used = t < nu_ref[0]

    @pl.when(used)
    def _():
        part = _dot(h_ref[...], w_ref[0].astype(BF16))

        @pl.when(k == 0)
        def _():
            acc_ref[...] = part

        @pl.when(k > 0)
        def _():
            acc_ref[...] += part

        @pl.when(k == nk - 1)
        def _():
            o_ref[...] = acc_ref[...] * gate_ref[...]

    @pl.when(jnp.logical_not(used))
    def _():
        o_ref[...] = jnp.zeros(o_ref.shape, o_ref.dtype)


def _moe_down(h_sorted, w_down, gate_sorted, tile_expert, n_used):
    p, f = h_sorted.shape
    d = w_down.shape[2]
    bm, bn, bk = MOE_TILE, _blk(d, 2048), _blk(f, 1024)
    nk = f // bk
    last = lambda t, nu: jnp.minimum(t, nu[0] - 1)
    kk = lambda t, k, nu: jnp.where(t < nu[0], k, nk - 1)
    grid_spec = pltpu.PrefetchScalarGridSpec(
        num_scalar_prefetch=2,
        grid=(p // bm, d // bn, nk),
        in_specs=[pl.BlockSpec((bm, bk), lambda t, j, k, te, nu: (last(t, nu), kk(t, k, nu))),
                  pl.BlockSpec((1, bk, bn), lambda t, j, k, te, nu: (te[t], kk(t, k, nu), j)),
                  pl.BlockSpec((bm, 1), lambda t, j, k, te, nu: (t, 0))],
        out_specs=pl.BlockSpec((bm, bn), lambda t, j, k, te, nu: (t, j)),
        scratch_shapes=[pltpu.VMEM((bm, bn), F32)],
    )
    return pl.pallas_call(
        functools.partial(_moe_down_kernel, nk=nk),
        grid_spec=grid_spec,
        out_shape=jax.ShapeDtypeStruct((p, d), F32),
        compiler_params=_params("arbitrary", "arbitrary", "arbitrary"),
        name="moe_down",
    )(tile_expert, n_used, h_sorted, w_down, gate_sorted)


def _route(idx, gate, n_experts):
    t = idx.shape[0]
    n_pairs = t * TOP_K
    n_tiles = (n_pairs + n_experts * (MOE_TILE - 1)) // MOE_TILE
    e_flat = idx.reshape(-1)
    onehot = (e_flat[:, None] == jnp.arange(n_experts, dtype=jnp.int32)[None, :]).astype(jnp.int32)
    csum = jnp.cumsum(onehot, axis=0)
    counts = csum[-1]
    rank = jnp.sum((csum - onehot) * onehot, axis=1)
    padded = (counts + MOE_TILE - 1) // MOE_TILE * MOE_TILE
    pend = jnp.cumsum(padded)
    pstart = pend - padded
    dest = pstart[e_flat] + rank
    token = jnp.arange(n_pairs, dtype=jnp.int32) // TOP_K
    src = jnp.zeros((n_tiles * MOE_TILE,), jnp.int32).at[dest].set(token)
    gate_sorted = jnp.zeros((n_tiles * MOE_TILE,), F32).at[dest].set(gate.reshape(-1))
    n_used = (pend[-1] // MOE_TILE).astype(jnp.int32)
    tile_row = jnp.minimum(jnp.arange(n_tiles, dtype=jnp.int32), n_used - 1) * MOE_TILE
    tile_expert = jnp.sum((tile_row[:, None] >= pend[None, :]).astype(jnp.int32), axis=1)
    return src, gate_sorted[:, None], dest.reshape(t, TOP_K), tile_expert.astype(jnp.int32), n_used.reshape(1)


def _rope_tables(pos, rope):
    inv = ROPE_THETA ** (-jnp.arange(0, rope, 2, dtype=F32) / rope)
    ang = pos.astype(F32)[:, None] * inv[None, :]
    cos, sin = jnp.cos(ang), jnp.sin(ang)
    return jnp.concatenate([cos, cos], axis=-1), jnp.concatenate([-sin, sin], axis=-1)


def kernel(x_prompt, x_sample, state_conv, cache_mla, page_table, ln_mix_g, ln_mix_b, ln_ffn_g, ln_ffn_b, conv_w_in, conv_w, conv_w_out, kv_w_down, kv_norm_g, w_uk, w_uv, q_w_down, q_norm_g, q_w_up, attn_w_out, ffn_w_gate, ffn_w_up, ffn_w_down, router_w, expert_w_gate, expert_w_up, expert_w_down):
    batch, seq, d = x_prompt.shape
    dec_batch, dec_seq, _ = x_sample.shape
    n_conv_layers, conv_w_len, dc = conv_w.shape
    lat, n_heads, nope = w_uk.shape
    v_head = w_uv.shape[2]
    rope = q_w_up.shape[3] - nope
    n_experts = router_w.shape[2]
    page = cache_mla.shape[1]
    past_len = page_table.shape[1] * page
    assert n_conv_layers == 1 and conv_w_len == 3 and dec_seq >= conv_w_len - 1 and seq >= conv_w_len - 1
    assert q_w_down.shape[0] == 1 and router_w.shape[0] == 1 and ffn_w_gate.shape[0] == 1
    t_p, t_s = batch * seq, dec_batch * dec_seq
    scale = float(nope + rope) ** -0.5

    x = jnp.concatenate([x_prompt.reshape(t_p, d), x_sample.reshape(t_s, d)], axis=0)
    pos = jnp.concatenate([jnp.tile(jnp.arange(seq), batch), jnp.tile(past_len + jnp.arange(dec_seq), dec_batch)])
    cos2, sin2 = _rope_tables(pos, rope)

    gb, u = _conv_in(x.astype(BF16), conv_w_in[0].astype(BF16))
    prev = state_conv[0].astype(F32)
    pad_rows = lambda a: jnp.pad(a, ((0, 0), (0, dec_seq - a.shape[1]), (0, 0))).reshape(t_s, dc)
    hist = (pad_rows(prev[:, 1:2]), pad_rows(prev))
    g_p = _conv_gate(u[:t_p], gb[:t_p], conv_w[0], None, seq_len=seq, rows=seq)
    g_s = _conv_gate(u[t_p:], gb[t_p:], conv_w[0], hist, seq_len=dec_seq, rows=_blk(t_s, 1024))
    mix = _matmul(jnp.concatenate([g_p, g_s], axis=0), conv_w_out[0].astype(BF16),
                  bm=1024, bn=1024, bk=d, out_dtype=F32)
    conv_prompt = u[:t_p].reshape(batch, seq, dc)[:, seq - 2:][None]
    conv_sample = u[t_p:].reshape(dec_batch, dec_seq, dc)[:, dec_seq - 2:][None]
    x, xb = _ln_residual(x, mix, ln_mix_g[0], ln_mix_b[0])

    hid = _glu(xb, ffn_w_gate[0].astype(BF16), ffn_w_up[0].astype(BF16))
    ffn = _matmul(hid, ffn_w_down[0].astype(BF16), bm=1024, bn=1024, bk=2048, out_dtype=F32)
    x, xb = _ln_residual(x, ffn, ln_ffn_g[0], ln_ffn_b[0])

    rows, rows_b = _kv_rows(xb, kv_w_down.astype(BF16), kv_norm_g, cos2, sin2)
    e = rows.shape[1]

    cq = _mm_rms(xb, q_w_down[0].astype(BF16), q_norm_g[0])
    w_nope = q_w_up[0, :, :, :nope].transpose(1, 0, 2).astype(BF16)
    w_rope = q_w_up[0, :, :, nope:].transpose(1, 0, 2).astype(BF16)
    w_uk_t = w_uk.transpose(1, 2, 0).astype(BF16)
    w_uv_t = w_uv.transpose(1, 0, 2).astype(BF16)
    heads = functools.partial(_q_heads, cq, w_nope, w_rope, w_uk_t, cos2, sin2, scale=scale)
    q_p = heads(row0=0, nrows=t_p, out_dtype=BF16)
    q_s = heads(row0=t_p, nrows=t_s, out_dtype=F32)
    o_p = _attn_prompt(q_p, rows_b, batch=batch, seq=seq, lat=lat)
    kv_new = jnp.pad(rows_b[t_p:].reshape(dec_batch, dec_seq, e), ((0, 0), (0, LANES - dec_seq), (0, 0)))
    o_s = _attn_paged(q_s, kv_new, cache_mla, page_table, lat=lat)
    o = jnp.concatenate([_o_heads(o_p, w_uv_t), _o_heads(o_s, w_uv_t)], axis=0)
    mix = _matmul(o, attn_w_out[0].astype(BF16), bm=1024, bn=1024, bk=n_heads * v_head, out_dtype=F32)
    x, xb = _ln_residual(x, mix, ln_mix_g[1], ln_mix_b[1])

    idx, gate = _router(x, router_w[0])
    src, gate_sorted, dest, tile_expert, n_used = _route(idx, gate, n_experts)
    hid = _moe_glu(xb[src], expert_w_gate[0], expert_w_up[0], tile_expert, n_used)
    y_sorted = _moe_down(hid, expert_w_down[0], gate_sorted, tile_expert, n_used)
    moe = y_sorted[dest[:, 0]] + y_sorted[dest[:, 1]]
    x, _ = _ln_residual(x, moe, ln_ffn_g[1], ln_ffn_b[1])

    return (x[:t_p].reshape(batch, seq, d), x[t_p:].reshape(dec_batch, dec_seq, d),
            conv_prompt, conv_sample,
            rows[:t_p].reshape(batch, seq, e), rows[t_p:].reshape(dec_batch, dec_seq, e))
```

```python
import functools

import jax
import jax.numpy as jnp
from jax import lax
from jax.experimental import pallas as pl
from jax.experimental.pallas import tpu as pltpu

F32 = jnp.float32
BF16 = jnp.bfloat16

DEPTH = 2
ALPHA = (2.0 * DEPTH) ** 0.25
NORM_EPS = 1e-5
ROPE_THETA = 10000.0
TOP_K = 2
LANES = 128
VMEM_LIMIT_BYTES = 56 * 1024 * 1024
MOE_TILE = 512
MOE_GLU_SUBCOLS = 256
MOE_DOWN_SUBCOLS = 512
PAGES_PER_SOFTMAX_STEP = 2
PAGES_PER_GRID_STEP = 32


def _params(*semantics):
    return pltpu.CompilerParams(dimension_semantics=semantics, vmem_limit_bytes=VMEM_LIMIT_BYTES)


def _blk(dim, pref):
    b = min(dim, pref)
    while dim % b:
        b //= 2
    return b


def _dot(a, b):
    return jnp.dot(a, b, preferred_element_type=F32)


def _rope(x, cos2, sin2):
    half = x.shape[-1] // 2
    rot = jnp.concatenate([x[:, half:], x[:, :half]], axis=-1)
    return x * cos2 + rot * sin2


def _mm_kernel(x_ref, w_ref, o_ref, *scratch, nk):
    part = _dot(x_ref[...], w_ref[...].astype(BF16))
    if nk == 1:
        o_ref[...] = part.astype(o_ref.dtype)
        return
    (acc_ref,) = scratch
    k = pl.program_id(2)

    @pl.when(k == 0)
    def _():
        acc_ref[...] = part

    @pl.when(k > 0)
    def _():
        acc_ref[...] += part

    @pl.when(k == nk - 1)
    def _():
        o_ref[...] = acc_ref[...].astype(o_ref.dtype)


def _matmul(x, w, *, bm, bn, bk, out_dtype):
    m, kdim = x.shape
    n = w.shape[1]
    bm, bn, bk = _blk(m, bm), _blk(n, bn), _blk(kdim, bk)
    nk = kdim // bk
    return pl.pallas_call(
        functools.partial(_mm_kernel, nk=nk),
        grid=(m // bm, n // bn, nk),
        in_specs=[pl.BlockSpec((bm, bk), lambda i, j, k: (i, k)),
                  pl.BlockSpec((bk, bn), lambda i, j, k: (k, j))],
        out_specs=pl.BlockSpec((bm, bn), lambda i, j, k: (i, j)),
        out_shape=jax.ShapeDtypeStruct((m, n), out_dtype),
        scratch_shapes=[pltpu.VMEM((bm, bn), F32)] if nk > 1 else [],
        compiler_params=_params("arbitrary", "arbitrary", "arbitrary"),
        name="matmul",
    )(x, w)


def _conv_in_kernel(x_ref, wb_ref, wc_ref, wh_ref, gb_ref, u_ref):
    x = x_ref[...]
    gb_ref[...] = _dot(x, wb_ref[...])
    u_ref[...] = _dot(x, wc_ref[...]) * _dot(x, wh_ref[...])


def _conv_in(x, w_in):
    m, d = x.shape
    dc = w_in.shape[1] // 3
    bm, bn = _blk(m, 1024), _blk(dc, 256)
    nb = dc // bn
    w_spec = lambda part: pl.BlockSpec((d, bn), lambda i, j: (0, j + part * nb))
    return pl.pallas_call(
        _conv_in_kernel,
        grid=(m // bm, nb),
        in_specs=[pl.BlockSpec((bm, d), lambda i, j: (i, 0)), w_spec(0), w_spec(1), w_spec(2)],
        out_specs=[pl.BlockSpec((bm, bn), lambda i, j: (i, j))] * 2,
        out_shape=[jax.ShapeDtypeStruct((m, dc), F32)] * 2,
        compiler_params=_params("arbitrary", "arbitrary"),
        name="conv_in",
    )(x, w_in, w_in, w_in)


def _conv_gate_kernel(*refs, seq_len, has_hist):
    if has_hist:
        u_ref, gb_ref, w_ref, h1_ref, h2_ref, g_ref = refs
    else:
        u_ref, gb_ref, w_ref, g_ref = refs
    u = u_ref[...]
    t = lax.broadcasted_iota(jnp.int32, u.shape, 0) % seq_len
    prev1 = h1_ref[...] if has_hist else 0.0
    prev2 = h2_ref[...] if has_hist else 0.0
    s1 = jnp.where(t >= 1, pltpu.roll(u, 1, axis=0), prev1)
    s2 = jnp.where(t >= 2, pltpu.roll(u, 2, axis=0), prev2)
    w = w_ref[...]
    v = w[0:1, :] * s2 + w[1:2, :] * s1 + w[2:3, :] * u
    g_ref[...] = (gb_ref[...] * v).astype(g_ref.dtype)


def _conv_gate(u, gb, w_conv, hist, *, seq_len, rows, row0, nrows):
    dc = u.shape[1]
    bc = _blk(dc, 256)
    assert row0 % rows == 0 and nrows % rows == 0 and rows % seq_len == 0
    r0 = row0 // rows
    blk = pl.BlockSpec((rows, bc), lambda i, j: (i, j))
    src = pl.BlockSpec((rows, bc), lambda i, j: (i + r0, j))
    ins = [u, gb, w_conv] + (list(hist) if hist is not None else [])
    in_specs = [src, src, pl.BlockSpec((w_conv.shape[0], bc), lambda i, j: (0, j))]
    in_specs += [blk, blk] if hist is not None else []
    return pl.pallas_call(
        functools.partial(_conv_gate_kernel, seq_len=seq_len, has_hist=hist is not None),
        grid=(nrows // rows, dc // bc),
        in_specs=in_specs,
        out_specs=blk,
        out_shape=jax.ShapeDtypeStruct((nrows, dc), BF16),
        compiler_params=_params("arbitrary", "arbitrary"),
        name="conv_gate",
    )(*ins)


def _ln_kernel(x_ref, *refs):
    *h_refs, g_ref, b_ref, y_ref, yb_ref = refs
    h = h_refs[0][...]
    for extra in h_refs[1:]:
        h = h + extra[...]
    z = ALPHA * x_ref[...] + h
    mu = jnp.mean(z, axis=-1, keepdims=True)
    zc = z - mu
    var = jnp.mean(zc * zc, axis=-1, keepdims=True)
    y = zc * lax.rsqrt(var + NORM_EPS) * g_ref[...] + b_ref[...]
    y_ref[...] = y
    yb_ref[...] = y.astype(BF16)


def _ln_residual(x, hs, g, b):
    m, d = x.shape
    bm = _blk(m, 256)
    row = pl.BlockSpec((bm, d), lambda i: (i, 0))
    vec = pl.BlockSpec((1, d), lambda i: (0, 0))
    return pl.pallas_call(
        _ln_kernel,
        grid=(m // bm,),
        in_specs=[row] * (1 + len(hs)) + [vec, vec],
        out_specs=[row, row],
        out_shape=[jax.ShapeDtypeStruct((m, d), F32), jax.ShapeDtypeStruct((m, d), BF16)],
        compiler_params=_params("arbitrary"),
        name="ln_residual",
    )(x, *hs, g.reshape(1, d), b.reshape(1, d))


def _glu_kernel(x_ref, wg_ref, wu_ref, o_ref):
    x = x_ref[...]
    g = _dot(x, wg_ref[...].astype(BF16))
    u = _dot(x, wu_ref[...].astype(BF16))
    o_ref[...] = (jax.nn.silu(g) * u).astype(o_ref.dtype)


def _glu(x, wg, wu):
    m, d = x.shape
    n = wg.shape[1]
    bm, bn = _blk(m, 1024), _blk(n, 512)
    w_spec = pl.BlockSpec((d, bn), lambda i, j: (0, j))
    return pl.pallas_call(
        _glu_kernel,
        grid=(m // bm, n // bn),
        in_specs=[pl.BlockSpec((bm, d), lambda i, j: (i, 0)), w_spec, w_spec],
        out_specs=pl.BlockSpec((bm, bn), lambda i, j: (i, j)),
        out_shape=jax.ShapeDtypeStruct((m, n), BF16),
        compiler_params=_params("arbitrary", "arbitrary"),
        name="glu",
    )(x, wg, wu)


def _kv_rows_kernel(x_ref, w_ref, g_ref, cos_ref, sin_ref, o_ref, ob_ref):
    kv_lora = g_ref.shape[-1]
    ckr = _dot(x_ref[...], w_ref[...])
    c = ckr[:, :kv_lora]
    c = c * lax.rsqrt(jnp.mean(c * c, axis=-1, keepdims=True) + NORM_EPS) * g_ref[...]
    kr = _rope(ckr[:, kv_lora:], cos_ref[...], sin_ref[...])
    out = jnp.concatenate([c, kr], axis=-1)
    o_ref[...] = out
    ob_ref[...] = out.astype(BF16)


def _kv_rows(x, w, g, cos2, sin2):
    m, d = x.shape
    e = w.shape[1]
    rope = cos2.shape[1]
    bm = _blk(m, 512)
    row = lambda width: pl.BlockSpec((bm, width), lambda i: (i, 0))
    return pl.pallas_call(
        _kv_rows_kernel,
        grid=(m // bm,),
        in_specs=[row(d), pl.BlockSpec((d, e), lambda i: (0, 0)),
                  pl.BlockSpec((1, e - rope), lambda i: (0, 0)), row(rope), row(rope)],
        out_specs=[row(e), row(e)],
        out_shape=[jax.ShapeDtypeStruct((m, e), F32), jax.ShapeDtypeStruct((m, e), BF16)],
        compiler_params=_params("arbitrary"),
        name="kv_rows",
    )(x, w, g.reshape(1, -1), cos2, sin2)


def _mm_rms_kernel(x_ref, w_ref, g_ref, o_ref):
    c = _dot(x_ref[...], w_ref[...])
    c = c * lax.rsqrt(jnp.mean(c * c, axis=-1, keepdims=True) + NORM_EPS) * g_ref[...]
    o_ref[...] = c.astype(o_ref.dtype)


def _mm_rms(x, w, g):
    m, d = x.shape
    n = w.shape[1]
    bm = _blk(m, 512)
    return pl.pallas_call(
        _mm_rms_kernel,
        grid=(m // bm,),
        in_specs=[pl.BlockSpec((bm, d), lambda i: (i, 0)), pl.BlockSpec((d, n), lambda i: (0, 0)),
                  pl.BlockSpec((1, n), lambda i: (0, 0))],
        out_specs=pl.BlockSpec((bm, n), lambda i: (i, 0)),
        out_shape=jax.ShapeDtypeStruct((m, n), BF16),
        compiler_params=_params("arbitrary"),
        name="q_down_rms",
    )(x, w, g.reshape(1, n))


def _q_heads_kernel(cq_ref, wn_ref, wr_ref, wuk_ref, cos_ref, sin_ref, o_ref, *, scale):
    cq = cq_ref[...]
    q_nope = _dot(cq, wn_ref[0])
    q_pe = _rope(_dot(cq, wr_ref[0]), cos_ref[...], sin_ref[...])
    q_lat = _dot(q_nope.astype(BF16), wuk_ref[0])
    o_ref[0] = (jnp.concatenate([q_lat, q_pe], axis=-1) * scale).astype(o_ref.dtype)


def _q_heads(cq, w_nope, w_rope, w_uk_t, cos2, sin2, *, row0, nrows, scale, out_dtype):
    ql = cq.shape[1]
    h, _, nope = w_nope.shape
    rope = w_rope.shape[2]
    lat = w_uk_t.shape[2]
    bm = _blk(nrows, 1024)
    assert row0 % bm == 0
    r0 = row0 // bm
    row = lambda width: pl.BlockSpec((bm, width), lambda i, j: (i + r0, 0))
    head = lambda a, b: pl.BlockSpec((1, a, b), lambda i, j: (j, 0, 0))
    return pl.pallas_call(
        functools.partial(_q_heads_kernel, scale=scale),
        grid=(nrows // bm, h),
        in_specs=[row(ql), head(ql, nope), head(ql, rope), head(nope, lat), row(rope), row(rope)],
        out_specs=pl.BlockSpec((1, bm, lat + rope), lambda i, j: (j, i, 0)),
        out_shape=jax.ShapeDtypeStruct((h, nrows, lat + rope), out_dtype),
        compiler_params=_params("arbitrary", "arbitrary"),
        name="q_heads",
    )(cq, w_nope, w_rope, w_uk_t, cos2, sin2)


def _softmax_step(s, v, m_ref, l_ref, acc_ref, v_transposed=False):
    m_prev = m_ref[...]
    m_new = jnp.maximum(m_prev, jnp.max(s, axis=-1, keepdims=True))
    alpha = jnp.exp(m_prev - m_new)
    p = jnp.exp(s - m_new[:, 0:1])
    l_ref[...] = alpha * l_ref[...] + jnp.sum(p, axis=-1, keepdims=True)
    pv = _scores(p.astype(BF16), v) if v_transposed else _dot(p.astype(BF16), v)
    acc_ref[...] = acc_ref[...] * alpha[:, 0:1] + pv
    m_ref[...] = m_new


def _softmax_init(m_ref, l_ref, acc_ref):
    m_ref[...] = jnp.full(m_ref.shape, -jnp.inf, F32)
    l_ref[...] = jnp.zeros(l_ref.shape, F32)
    acc_ref[...] = jnp.zeros(acc_ref.shape, F32)


def _scores(q, k):
    return lax.dot_general(q, k, (((1,), (1,)), ((), ())), preferred_element_type=F32)


def _attn_prompt_kernel(q_ref, kv_ref, o_ref, m_ref, l_ref, acc_ref, *, blk):
    qi = pl.program_id(1)
    hg, _, e = q_ref.shape
    lat = o_ref.shape[-1]
    rows = hg * blk
    q = q_ref[...].reshape(rows, e)
    _softmax_init(m_ref, l_ref, acc_ref)
    def keys(j):
        return kv_ref[pl.ds(pl.multiple_of(j * blk, blk), blk), :]

    def body(j, s):
        s_next = _scores(q, keys(j + 1))
        _softmax_step(s, keys(j)[:, :lat], m_ref, l_ref, acc_ref)
        return s_next

    s = lax.fori_loop(0, qi, body, _scores(q, keys(0)))
    q_off = lax.broadcasted_iota(jnp.int32, (rows, blk), 0) % blk
    k_off = lax.broadcasted_iota(jnp.int32, (rows, blk), 1)
    s = jnp.where(k_off <= q_off, s, -jnp.inf)
    _softmax_step(s, keys(qi)[:, :lat], m_ref, l_ref, acc_ref)
    out = acc_ref[...] / l_ref[...][:, 0:1]
    o_ref[...] = out.reshape(hg, blk, lat).astype(o_ref.dtype)


def _attn_prompt(q, kv, *, batch, seq, lat):
    h, _, e = q.shape
    blk = _blk(seq, 256)
    hg = _blk(h, 4)
    nq = seq // blk
    return pl.pallas_call(
        functools.partial(_attn_prompt_kernel, blk=blk),
        grid=(batch, nq, h // hg),
        in_specs=[pl.BlockSpec((hg, blk, e), lambda b, i, g: (g, b * nq + i, 0)),
                  pl.BlockSpec((seq, e), lambda b, i, g: (b, 0))],
        out_specs=pl.BlockSpec((hg, blk, lat), lambda b, i, g: (g, b * nq + i, 0)),
        out_shape=jax.ShapeDtypeStruct((h, batch * seq, lat), BF16),
        scratch_shapes=[pltpu.VMEM((hg * blk, LANES), F32), pltpu.VMEM((hg * blk, LANES), F32),
                        pltpu.VMEM((hg * blk, lat), F32)],
        compiler_params=_params("arbitrary", "arbitrary", "arbitrary"),
        name="attn_prompt",
    )(q, kv)


def _attn_paged_kernel(pt_ref, q_ref, new_ref, *rest, n_par, n_new):
    del pt_ref
    cache_refs = rest[:n_par]
    o_ref, m_ref, l_ref, acc_ref = rest[n_par:]
    g = pl.program_id(1)
    h, _, e = q_ref.shape
    lat = o_ref.shape[-1]
    rows = h * n_new
    q = q_ref[...].reshape(rows, e).astype(BF16)

    @pl.when(g == 0)
    def _():
        _softmax_init(m_ref, l_ref, acc_ref)

    size = _blk(n_par, PAGES_PER_SOFTMAX_STEP)
    groups = [cache_refs[i:i + size] for i in range(0, n_par, size)]
    k_ts = [jnp.concatenate([c[0].astype(BF16) for c in grp], axis=1) for grp in groups]
    ss = [_dot(q, k_t) for k_t in k_ts]
    for s, k_t in zip(ss, k_ts):
        _softmax_step(s, k_t[:lat, :], m_ref, l_ref, acc_ref, v_transposed=True)

    @pl.when(g == pl.num_programs(1) - 1)
    def _():
        kn = new_ref[0]
        q_pos = lax.broadcasted_iota(jnp.int32, (rows, kn.shape[0]), 0) % n_new
        k_pos = lax.broadcasted_iota(jnp.int32, (rows, kn.shape[0]), 1)
        s = jnp.where(k_pos <= q_pos, _scores(q, kn), -jnp.inf)
        _softmax_step(s, kn[:, :lat], m_ref, l_ref, acc_ref)
        out = acc_ref[...] / l_ref[...][:, 0:1]
        o_ref[...] = out.reshape(h, n_new, lat)


def _attn_paged(q, kv_new_pad, cache_t, page_table, *, lat):
    h, t, e = q.shape
    b, n_pages = page_table.shape
    n_new = t // b
    page = cache_t.shape[2]
    n_par = _blk(n_pages, PAGES_PER_GRID_STEP)

    def cache_spec(p):
        return pl.BlockSpec((1, e, page), lambda i, g, pt: (pt[i * n_pages + g * n_par + p], 0, 0))

    grid_spec = pltpu.PrefetchScalarGridSpec(
        num_scalar_prefetch=1,
        grid=(b, n_pages // n_par),
        in_specs=[pl.BlockSpec((h, n_new, e), lambda i, g, pt: (0, i, 0)),
                  pl.BlockSpec((1, kv_new_pad.shape[1], e), lambda i, g, pt: (i, 0, 0))]
                 + [cache_spec(p) for p in range(n_par)],
        out_specs=pl.BlockSpec((h, n_new, lat), lambda i, g, pt: (0, i, 0)),
        scratch_shapes=[pltpu.VMEM((h * n_new, LANES), F32), pltpu.VMEM((h * n_new, LANES), F32),
                        pltpu.VMEM((h * n_new, lat), F32)],
    )
    return pl.pallas_call(
        functools.partial(_attn_paged_kernel, n_par=n_par, n_new=n_new),
        grid_spec=grid_spec,
        out_shape=jax.ShapeDtypeStruct((h, t, lat), F32),
        compiler_params=_params("arbitrary", "arbitrary"),
        name="attn_paged",
    )(page_table.reshape(-1), q, kv_new_pad, *([cache_t] * n_par))


def _o_heads_kernel(ol_ref, w_ref, o_ref):
    o_ref[...] = _dot(ol_ref[0].astype(BF16), w_ref[0]).astype(o_ref.dtype)


def _o_heads(o_lat, w_uv_t):
    h, m, lat = o_lat.shape
    v = w_uv_t.shape[2]
    bm = _blk(m, 1024)
    return pl.pallas_call(
        _o_heads_kernel,
        grid=(m // bm, h),
        in_specs=[pl.BlockSpec((1, bm, lat), lambda i, j: (j, i, 0)),
                  pl.BlockSpec((1, lat, v), lambda i, j: (j, 0, 0))],
        out_specs=pl.BlockSpec((bm, v), lambda i, j: (i, j)),
        out_shape=jax.ShapeDtypeStruct((m, h * v), BF16),
        compiler_params=_params("arbitrary", "arbitrary"),
        name="o_heads",
    )(o_lat, w_uv_t)


def _router_kernel(x_ref, w_ref, o_ref, *, n_experts):
    logits = jnp.dot(x_ref[...], w_ref[...], precision=lax.Precision.HIGHEST, preferred_element_type=F32)
    lane = lax.broadcasted_iota(jnp.int32, logits.shape, 1)
    logits = jnp.where(lane < n_experts, logits, -jnp.inf)
    v1 = jnp.max(logits, axis=-1, keepdims=True)
    i1 = jnp.min(jnp.where(logits == v1, lane, LANES), axis=-1, keepdims=True)
    rest = jnp.where(lane == i1, -jnp.inf, logits)
    v2 = jnp.max(rest, axis=-1, keepdims=True)
    i2 = jnp.min(jnp.where(rest == v2, lane, LANES), axis=-1, keepdims=True)
    e2 = jnp.exp(v2 - v1)
    denom = 1.0 + e2
    out = jnp.where(lane == 0, i1.astype(F32),
                    jnp.where(lane == 1, i2.astype(F32),
                              jnp.where(lane == 2, 1.0 / denom, e2 / denom)))
    o_ref[...] = out


def _router(x, router_w):
    m, d = x.shape
    n_experts = router_w.shape[1]
    w = jnp.pad(router_w, ((0, 0), (0, LANES - n_experts)))
    bm = _blk(m, 512)
    meta = pl.pallas_call(
        functools.partial(_router_kernel, n_experts=n_experts),
        grid=(m // bm,),
        in_specs=[pl.BlockSpec((bm, d), lambda i: (i, 0)), pl.BlockSpec((d, LANES), lambda i: (0, 0))],
        out_specs=pl.BlockSpec((bm, LANES), lambda i: (i, 0)),
        out_shape=jax.ShapeDtypeStruct((m, LANES), F32),
        compiler_params=_params("arbitrary"),
        name="router",
    )(x, w)
    return meta[:, :TOP_K].astype(jnp.int32), meta[:, TOP_K:2 * TOP_K]


def _moe_glu_kernel(te_ref, nu_ref, x_ref, wg_ref, wu_ref, o_ref):
    del te_ref
    t = pl.program_id(1)

    @pl.when(t < nu_ref[0])
    def _():
        x = x_ref[...]
        bn = o_ref.shape[1]
        sub = _blk(bn, MOE_GLU_SUBCOLS)
        for c in range(0, bn, sub):
            g = _dot(x, wg_ref[0, :, c:c + sub].astype(BF16))
            u = _dot(x, wu_ref[0, :, c:c + sub].astype(BF16))
            o_ref[:, c:c + sub] = (jax.nn.silu(g) * u).astype(o_ref.dtype)

    @pl.when(t >= nu_ref[0])
    def _():
        o_ref[...] = jnp.zeros(o_ref.shape, o_ref.dtype)


def _moe_glu(x_sorted, w_gate, w_up, tile_expert, n_used):
    p, d = x_sorted.shape
    n = w_gate.shape[2]
    bm, bn = MOE_TILE, _blk(n, 512)
    last = lambda t, nu: jnp.minimum(t, nu[0] - 1)
    w_spec = pl.BlockSpec((1, d, bn), lambda j, t, te, nu: (te[t], 0, j))
    grid_spec = pltpu.PrefetchScalarGridSpec(
        num_scalar_prefetch=2,
        grid=(n // bn, p // bm),
        in_specs=[pl.BlockSpec((bm, d), lambda j, t, te, nu: (last(t, nu), 0)), w_spec, w_spec],
        out_specs=pl.BlockSpec((bm, bn), lambda j, t, te, nu: (t, j)),
    )
    return pl.pallas_call(
        _moe_glu_kernel,
        grid_spec=grid_spec,
        out_shape=jax.ShapeDtypeStruct((p, n), BF16),
        compiler_params=_params("arbitrary", "arbitrary"),
        name="moe_glu",
    )(tile_expert, n_used, x_sorted, w_gate, w_up)


def _moe_down_kernel(te_ref, nu_ref, h_ref, w_ref, gate_ref, o_ref, acc_ref, *, nk):
    del te_ref
    t = pl.program_id(0)
    k = pl.program_id(2)
    used = t < nu_ref[0]

    @pl.when(jnp.logical_and(used, k == 0))
    def _():
        acc_ref[...] = jnp.zeros(acc_ref.shape, F32)

    @pl.when(used)
    def _():
        h = h_ref[...]
        bn = acc_ref.shape[1]
        sub = _blk(bn, MOE_DOWN_SUBCOLS)
        for c in range(0, bn, sub):
            acc_ref[:, c:c + sub] += _dot(h, w_ref[0, :, c:c + sub].astype(BF16))

        @pl.when(k == nk - 1)
        def _():
            o_ref[...] = acc_ref[...] * gate_ref[...]

    @pl.when(jnp.logical_not(used))
    def _():
        o_ref[...] = jnp.zeros(o_ref.shape, o_ref.dtype)


def _moe_down(h_sorted, w_down, gate_sorted, tile_expert, n_used):
    p, f = h_sorted.shape
    d = w_down.shape[2]
    bm, bn, bk = MOE_TILE, _blk(d, 2048), _blk(f, 1024)
    nk = f // bk
    last = lambda t, nu: jnp.minimum(t, nu[0] - 1)
    kk = lambda t, k, nu: jnp.where(t < nu[0], k, nk - 1)
    grid_spec = pltpu.PrefetchScalarGridSpec(
        num_scalar_prefetch=2,
        grid=(p // bm, d // bn, nk),
        in_specs=[pl.BlockSpec((bm, bk), lambda t, j, k, te, nu: (last(t, nu), kk(t, k, nu))),
                  pl.BlockSpec((1, bk, bn), lambda t, j, k, te, nu: (te[t], kk(t, k, nu), j)),
                  pl.BlockSpec((bm, 1), lambda t, j, k, te, nu: (t, 0))],
        out_specs=pl.BlockSpec((bm, bn), lambda t, j, k, te, nu: (t, j)),
        scratch_shapes=[pltpu.VMEM((bm, bn), F32)],
    )
    return pl.pallas_call(
        functools.partial(_moe_down_kernel, nk=nk),
        grid_spec=grid_spec,
        out_shape=jax.ShapeDtypeStruct((p, d), F32),
        compiler_params=_params("arbitrary", "arbitrary", "arbitrary"),
        name="moe_down",
    )(tile_expert, n_used, h_sorted, w_down, gate_sorted)


def _route(idx, gate, n_experts):
    t = idx.shape[0]
    n_pairs = t * TOP_K
    n_tiles = (n_pairs + n_experts * (MOE_TILE - 1)) // MOE_TILE
    e_flat = idx.reshape(-1)
    onehot = (e_flat[:, None] == jnp.arange(n_experts, dtype=jnp.int32)[None, :]).astype(jnp.int32)
    csum = jnp.cumsum(onehot, axis=0)
    counts = csum[-1]
    rank = jnp.sum((csum - onehot) * onehot, axis=1)
    padded = (counts + MOE_TILE - 1) // MOE_TILE * MOE_TILE
    pend = jnp.cumsum(padded)
    pstart = pend - padded
    dest = pstart[e_flat] + rank
    token = jnp.arange(n_pairs, dtype=jnp.int32) // TOP_K
    src = jnp.zeros((n_tiles * MOE_TILE,), jnp.int32).at[dest].set(token)
    gate_sorted = jnp.zeros((n_tiles * MOE_TILE,), F32).at[dest].set(gate.reshape(-1))
    n_used = (pend[-1] // MOE_TILE).astype(jnp.int32)
    tile_row = jnp.minimum(jnp.arange(n_tiles, dtype=jnp.int32), n_used - 1) * MOE_TILE
    tile_expert = jnp.sum((tile_row[:, None] >= pend[None, :]).astype(jnp.int32), axis=1)
    return src, gate_sorted[:, None], dest.reshape(t, TOP_K), tile_expert.astype(jnp.int32), n_used.reshape(1)


def _rope_tables(pos, rope):
    inv = ROPE_THETA ** (-jnp.arange(0, rope, 2, dtype=F32) / rope)
    ang = pos.astype(F32)[:, None] * inv[None, :]
    cos, sin = jnp.cos(ang), jnp.sin(ang)
    return jnp.concatenate([cos, cos], axis=-1), jnp.concatenate([-sin, sin], axis=-1)


def kernel(x_prompt, x_sample, state_conv, cache_mla, page_table, ln_mix_g, ln_mix_b, ln_ffn_g, ln_ffn_b, conv_w_in, conv_w, conv_w_out, kv_w_down, kv_norm_g, w_uk, w_uv, q_w_down, q_norm_g, q_w_up, attn_w_out, ffn_w_gate, ffn_w_up, ffn_w_down, router_w, expert_w_gate, expert_w_up, expert_w_down):
    batch, seq, d = x_prompt.shape
    dec_batch, dec_seq, _ = x_sample.shape
    n_conv_layers, conv_w_len, dc = conv_w.shape
    lat, n_heads, nope = w_uk.shape
    v_head = w_uv.shape[2]
    rope = q_w_up.shape[3] - nope
    n_experts = router_w.shape[2]
    page = cache_mla.shape[1]
    past_len = page_table.shape[1] * page
    assert n_conv_layers == 1 and conv_w_len == 3 and dec_seq >= conv_w_len - 1 and seq >= conv_w_len - 1
    assert q_w_down.shape[0] == 1 and router_w.shape[0] == 1 and ffn_w_gate.shape[0] == 1
    t_p, t_s = batch * seq, dec_batch * dec_seq
    scale = float(nope + rope) ** -0.5

    x = jnp.concatenate([x_prompt.reshape(t_p, d), x_sample.reshape(t_s, d)], axis=0)
    pos = jnp.concatenate([jnp.tile(jnp.arange(seq), batch), jnp.tile(past_len + jnp.arange(dec_seq), dec_batch)])
    cos2, sin2 = _rope_tables(pos, rope)

    gb, u = _conv_in(x.astype(BF16), conv_w_in[0].astype(BF16))
    prev = state_conv[0].astype(F32)
    pad_rows = lambda a: jnp.pad(a, ((0, 0), (0, dec_seq - a.shape[1]), (0, 0))).reshape(t_s, dc)
    hist = (pad_rows(prev[:, 1:2]), pad_rows(prev))
    g_p = _conv_gate(u, gb, conv_w[0], None, seq_len=seq, rows=seq, row0=0, nrows=t_p)
    g_s = _conv_gate(u, gb, conv_w[0], hist, seq_len=dec_seq, rows=_blk(t_s, 1024), row0=t_p, nrows=t_s)
    mix = _matmul(jnp.concatenate([g_p, g_s], axis=0), conv_w_out[0].astype(BF16),
                  bm=1024, bn=1024, bk=d, out_dtype=F32)
    conv_prompt = u[:t_p].reshape(batch, seq, dc)[:, seq - 2:][None]
    conv_sample = u[t_p:].reshape(dec_batch, dec_seq, dc)[:, dec_seq - 2:][None]
    x, xb = _ln_residual(x, (mix,), ln_mix_g[0], ln_mix_b[0])

    hid = _glu(xb, ffn_w_gate[0].astype(BF16), ffn_w_up[0].astype(BF16))
    ffn = _matmul(hid, ffn_w_down[0].astype(BF16), bm=1024, bn=1024, bk=2048, out_dtype=F32)
    x, xb = _ln_residual(x, (ffn,), ln_ffn_g[0], ln_ffn_b[0])

    rows, rows_b = _kv_rows(xb, kv_w_down.astype(BF16), kv_norm_g, cos2, sin2)
    e = rows.shape[1]

    cq = _mm_rms(xb, q_w_down[0].astype(BF16), q_norm_g[0])
    w_nope = q_w_up[0, :, :, :nope].transpose(1, 0, 2).astype(BF16)
    w_rope = q_w_up[0, :, :, nope:].transpose(1, 0, 2).astype(BF16)
    w_uk_t = w_uk.transpose(1, 2, 0).astype(BF16)
    w_uv_t = w_uv.transpose(1, 0, 2).astype(BF16)
    heads = functools.partial(_q_heads, cq, w_nope, w_rope, w_uk_t, cos2, sin2, scale=scale)
    q_p = heads(row0=0, nrows=t_p, out_dtype=BF16)
    q_s = heads(row0=t_p, nrows=t_s, out_dtype=F32)
    o_p = _attn_prompt(q_p, rows_b, batch=batch, seq=seq, lat=lat)
    kv_new = jnp.pad(rows_b[t_p:].reshape(dec_batch, dec_seq, e), ((0, 0), (0, LANES - dec_seq), (0, 0)))
    o_s = _attn_paged(q_s, kv_new, jnp.swapaxes(cache_mla, 1, 2), page_table, lat=lat)
    o = jnp.concatenate([_o_heads(o_p, w_uv_t), _o_heads(o_s, w_uv_t)], axis=0)
    mix = _matmul(o, attn_w_out[0].astype(BF16), bm=1024, bn=1024, bk=n_heads * v_head, out_dtype=F32)
    x, xb = _ln_residual(x, (mix,), ln_mix_g[1], ln_mix_b[1])

    idx, gate = _router(x, router_w[0])
    src, gate_sorted, dest, tile_expert, n_used = _route(idx, gate, n_experts)
    hid = _moe_glu(xb[src], expert_w_gate[0], expert_w_up[0], tile_expert, n_used)
    y_sorted = _moe_down(hid, expert_w_down[0], gate_sorted, tile_expert, n_used)
    x, _ = _ln_residual(x, (y_sorted[dest[:, 0]], y_sorted[dest[:, 1]]), ln_ffn_g[1], ln_ffn_b[1])

    return (x[:t_p].reshape(batch, seq, d), x[t_p:].reshape(dec_batch, dec_seq, d),
            conv_prompt, conv_sample,
            rows[:t_p].reshape(batch, seq, e), rows[t_p:].reshape(dec_batch, dec_seq, e))
```

```python
import functools

import jax
import jax.numpy as jnp
from jax import lax
from jax.experimental import pallas as pl
from jax.experimental.pallas import tpu as pltpu

F32 = jnp.float32
BF16 = jnp.bfloat16

DEPTH = 2
ALPHA = (2.0 * DEPTH) ** 0.25
NORM_EPS = 1e-5
ROPE_THETA = 10000.0
TOP_K = 2
LANES = 128
VMEM_LIMIT_BYTES = 56 * 1024 * 1024
MOE_TILE = 512
MOE_GROUP = 2 * MOE_TILE
MOE_GLU_SUBCOLS = 256
MOE_DOWN_SUBCOLS = 512
PAGES_PER_SOFTMAX_STEP = 2
PAGES_PER_GRID_STEP = 32


def _params(*semantics):
    return pltpu.CompilerParams(dimension_semantics=semantics, vmem_limit_bytes=VMEM_LIMIT_BYTES)


def _blk(dim, pref):
    b = min(dim, pref)
    while dim % b:
        b //= 2
    return b


def _dot(a, b):
    return jnp.dot(a, b, preferred_element_type=F32)


def _rope(x, cos2, sin2):
    half = x.shape[-1] // 2
    rot = jnp.concatenate([x[:, half:], x[:, :half]], axis=-1)
    return x * cos2 + rot * sin2


def _mm_kernel(x_ref, w_ref, o_ref, *scratch, nk):
    part = _dot(x_ref[...], w_ref[...].astype(BF16))
    if nk == 1:
        o_ref[...] = part.astype(o_ref.dtype)
        return
    (acc_ref,) = scratch
    k = pl.program_id(2)

    @pl.when(k == 0)
    def _():
        acc_ref[...] = part

    @pl.when(k > 0)
    def _():
        acc_ref[...] += part

    @pl.when(k == nk - 1)
    def _():
        o_ref[...] = acc_ref[...].astype(o_ref.dtype)


def _matmul(x, w, *, bm, bn, bk, out_dtype):
    m, kdim = x.shape
    n = w.shape[1]
    bm, bn, bk = _blk(m, bm), _blk(n, bn), _blk(kdim, bk)
    nk = kdim // bk
    return pl.pallas_call(
        functools.partial(_mm_kernel, nk=nk),
        grid=(m // bm, n // bn, nk),
        in_specs=[pl.BlockSpec((bm, bk), lambda i, j, k: (i, k)),
                  pl.BlockSpec((bk, bn), lambda i, j, k: (k, j))],
        out_specs=pl.BlockSpec((bm, bn), lambda i, j, k: (i, j)),
        out_shape=jax.ShapeDtypeStruct((m, n), out_dtype),
        scratch_shapes=[pltpu.VMEM((bm, bn), F32)] if nk > 1 else [],
        compiler_params=_params("arbitrary", "arbitrary", "arbitrary"),
        name="matmul",
    )(x, w)


def _conv_in_kernel(x_ref, wb_ref, wc_ref, wh_ref, gb_ref, u_ref):
    x = x_ref[...]
    gb_ref[...] = _dot(x, wb_ref[...])
    u_ref[...] = _dot(x, wc_ref[...]) * _dot(x, wh_ref[...])


def _conv_in(x, w_in):
    m, d = x.shape
    dc = w_in.shape[1] // 3
    bm, bn = _blk(m, 1024), _blk(dc, 256)
    nb = dc // bn
    w_spec = lambda part: pl.BlockSpec((d, bn), lambda i, j: (0, j + part * nb))
    return pl.pallas_call(
        _conv_in_kernel,
        grid=(m // bm, nb),
        in_specs=[pl.BlockSpec((bm, d), lambda i, j: (i, 0)), w_spec(0), w_spec(1), w_spec(2)],
        out_specs=[pl.BlockSpec((bm, bn), lambda i, j: (i, j))] * 2,
        out_shape=[jax.ShapeDtypeStruct((m, dc), F32)] * 2,
        compiler_params=_params("arbitrary", "arbitrary"),
        name="conv_in",
    )(x, w_in, w_in, w_in)


def _conv_gate_kernel(*refs, seq_len, has_hist):
    if has_hist:
        u_ref, gb_ref, w_ref, h1_ref, h2_ref, g_ref = refs
    else:
        u_ref, gb_ref, w_ref, g_ref = refs
    u = u_ref[...]
    t = lax.broadcasted_iota(jnp.int32, u.shape, 0) % seq_len
    prev1 = h1_ref[...] if has_hist else 0.0
    prev2 = h2_ref[...] if has_hist else 0.0
    s1 = jnp.where(t >= 1, pltpu.roll(u, 1, axis=0), prev1)
    s2 = jnp.where(t >= 2, pltpu.roll(u, 2, axis=0), prev2)
    w = w_ref[...]
    v = w[0:1, :] * s2 + w[1:2, :] * s1 + w[2:3, :] * u
    g_ref[...] = (gb_ref[...] * v).astype(g_ref.dtype)


def _conv_gate(u, gb, w_conv, hist, *, seq_len, rows, row0, nrows):
    dc = u.shape[1]
    bc = _blk(dc, 256)
    assert row0 % rows == 0 and nrows % rows == 0 and rows % seq_len == 0
    r0 = row0 // rows
    blk = pl.BlockSpec((rows, bc), lambda i, j: (i, j))
    src = pl.BlockSpec((rows, bc), lambda i, j: (i + r0, j))
    ins = [u, gb, w_conv] + (list(hist) if hist is not None else [])
    in_specs = [src, src, pl.BlockSpec((w_conv.shape[0], bc), lambda i, j: (0, j))]
    in_specs += [blk, blk] if hist is not None else []
    return pl.pallas_call(
        functools.partial(_conv_gate_kernel, seq_len=seq_len, has_hist=hist is not None),
        grid=(nrows // rows, dc // bc),
        in_specs=in_specs,
        out_specs=blk,
        out_shape=jax.ShapeDtypeStruct((nrows, dc), BF16),
        compiler_params=_params("arbitrary", "arbitrary"),
        name="conv_gate",
    )(*ins)


def _ln_kernel(x_ref, *refs):
    *h_refs, g_ref, b_ref, y_ref, yb_ref = refs
    h = h_refs[0][...]
    for extra in h_refs[1:]:
        h = h + extra[...]
    z = ALPHA * x_ref[...] + h
    mu = jnp.mean(z, axis=-1, keepdims=True)
    zc = z - mu
    var = jnp.mean(zc * zc, axis=-1, keepdims=True)
    y = zc * lax.rsqrt(var + NORM_EPS) * g_ref[...] + b_ref[...]
    y_ref[...] = y
    yb_ref[...] = y.astype(BF16)


def _ln_residual(x, hs, g, b):
    m, d = x.shape
    bm = _blk(m, 256)
    row = pl.BlockSpec((bm, d), lambda i: (i, 0))
    vec = pl.BlockSpec((1, d), lambda i: (0, 0))
    return pl.pallas_call(
        _ln_kernel,
        grid=(m // bm,),
        in_specs=[row] * (1 + len(hs)) + [vec, vec],
        out_specs=[row, row],
        out_shape=[jax.ShapeDtypeStruct((m, d), F32), jax.ShapeDtypeStruct((m, d), BF16)],
        compiler_params=_params("arbitrary"),
        name="ln_residual",
    )(x, *hs, g.reshape(1, d), b.reshape(1, d))


def _glu_kernel(x_ref, wg_ref, wu_ref, o_ref):
    x = x_ref[...]
    g = _dot(x, wg_ref[...].astype(BF16))
    u = _dot(x, wu_ref[...].astype(BF16))
    o_ref[...] = (jax.nn.silu(g) * u).astype(o_ref.dtype)


def _glu(x, wg, wu):
    m, d = x.shape
    n = wg.shape[1]
    bm, bn = _blk(m, 1024), _blk(n, 512)
    w_spec = pl.BlockSpec((d, bn), lambda i, j: (0, j))
    return pl.pallas_call(
        _glu_kernel,
        grid=(m // bm, n // bn),
        in_specs=[pl.BlockSpec((bm, d), lambda i, j: (i, 0)), w_spec, w_spec],
        out_specs=pl.BlockSpec((bm, bn), lambda i, j: (i, j)),
        out_shape=jax.ShapeDtypeStruct((m, n), BF16),
        compiler_params=_params("arbitrary", "arbitrary"),
        name="glu",
    )(x, wg, wu)


def _kv_rows_kernel(x_ref, w_ref, g_ref, cos_ref, sin_ref, o_ref, ob_ref):
    kv_lora = g_ref.shape[-1]
    ckr = _dot(x_ref[...], w_ref[...])
    c = ckr[:, :kv_lora]
    c = c * lax.rsqrt(jnp.mean(c * c, axis=-1, keepdims=True) + NORM_EPS) * g_ref[...]
    kr = _rope(ckr[:, kv_lora:], cos_ref[...], sin_ref[...])
    out = jnp.concatenate([c, kr], axis=-1)
    o_ref[...] = out
    ob_ref[...] = out.astype(BF16)


def _kv_rows(x, w, g, cos2, sin2):
    m, d = x.shape
    e = w.shape[1]
    rope = cos2.shape[1]
    bm = _blk(m, 512)
    row = lambda width: pl.BlockSpec((bm, width), lambda i: (i, 0))
    return pl.pallas_call(
        _kv_rows_kernel,
        grid=(m // bm,),
        in_specs=[row(d), pl.BlockSpec((d, e), lambda i: (0, 0)),
                  pl.BlockSpec((1, e - rope), lambda i: (0, 0)), row(rope), row(rope)],
        out_specs=[row(e), row(e)],
        out_shape=[jax.ShapeDtypeStruct((m, e), F32), jax.ShapeDtypeStruct((m, e), BF16)],
        compiler_params=_params("arbitrary"),
        name="kv_rows",
    )(x, w, g.reshape(1, -1), cos2, sin2)


def _mm_rms_kernel(x_ref, w_ref, g_ref, o_ref):
    c = _dot(x_ref[...], w_ref[...])
    c = c * lax.rsqrt(jnp.mean(c * c, axis=-1, keepdims=True) + NORM_EPS) * g_ref[...]
    o_ref[...] = c.astype(o_ref.dtype)


def _mm_rms(x, w, g):
    m, d = x.shape
    n = w.shape[1]
    bm = _blk(m, 512)
    return pl.pallas_call(
        _mm_rms_kernel,
        grid=(m // bm,),
        in_specs=[pl.BlockSpec((bm, d), lambda i: (i, 0)), pl.BlockSpec((d, n), lambda i: (0, 0)),
                  pl.BlockSpec((1, n), lambda i: (0, 0))],
        out_specs=pl.BlockSpec((bm, n), lambda i: (i, 0)),
        out_shape=jax.ShapeDtypeStruct((m, n), BF16),
        compiler_params=_params("arbitrary"),
        name="q_down_rms",
    )(x, w, g.reshape(1, n))


def _q_heads_kernel(cq_ref, wn_ref, wr_ref, wuk_ref, cos_ref, sin_ref, o_ref, *, scale):
    cq = cq_ref[...]
    q_nope = _dot(cq, wn_ref[0])
    q_pe = _rope(_dot(cq, wr_ref[0]), cos_ref[...], sin_ref[...])
    q_lat = _dot(q_nope.astype(BF16), wuk_ref[0])
    o_ref[0] = (jnp.concatenate([q_lat, q_pe], axis=-1) * scale).astype(o_ref.dtype)


def _q_heads(cq, w_nope, w_rope, w_uk_t, cos2, sin2, *, row0, nrows, scale, out_dtype):
    ql = cq.shape[1]
    h, _, nope = w_nope.shape
    rope = w_rope.shape[2]
    lat = w_uk_t.shape[2]
    bm = _blk(nrows, 1024)
    assert row0 % bm == 0
    r0 = row0 // bm
    row = lambda width: pl.BlockSpec((bm, width), lambda i, j: (i + r0, 0))
    head = lambda a, b: pl.BlockSpec((1, a, b), lambda i, j: (j, 0, 0))
    return pl.pallas_call(
        functools.partial(_q_heads_kernel, scale=scale),
        grid=(nrows // bm, h),
        in_specs=[row(ql), head(ql, nope), head(ql, rope), head(nope, lat), row(rope), row(rope)],
        out_specs=pl.BlockSpec((1, bm, lat + rope), lambda i, j: (j, i, 0)),
        out_shape=jax.ShapeDtypeStruct((h, nrows, lat + rope), out_dtype),
        compiler_params=_params("arbitrary", "arbitrary"),
        name="q_heads",
    )(cq, w_nope, w_rope, w_uk_t, cos2, sin2)


def _softmax_step(s, v, m_ref, l_ref, acc_ref, v_transposed=False):
    m_prev = m_ref[...]
    m_new = jnp.maximum(m_prev, jnp.max(s, axis=-1, keepdims=True))
    alpha = jnp.exp(m_prev - m_new)
    p = jnp.exp(s - m_new[:, 0:1])
    l_ref[...] = alpha * l_ref[...] + jnp.sum(p, axis=-1, keepdims=True)
    pv = _scores(p.astype(BF16), v) if v_transposed else _dot(p.astype(BF16), v)
    acc_ref[...] = acc_ref[...] * alpha[:, 0:1] + pv
    m_ref[...] = m_new


def _softmax_init(m_ref, l_ref, acc_ref):
    m_ref[...] = jnp.full(m_ref.shape, -jnp.inf, F32)
    l_ref[...] = jnp.zeros(l_ref.shape, F32)
    acc_ref[...] = jnp.zeros(acc_ref.shape, F32)


def _scores(q, k):
    return lax.dot_general(q, k, (((1,), (1,)), ((), ())), preferred_element_type=F32)


def _attn_prompt_kernel(q_ref, kv_ref, o_ref, m_ref, l_ref, acc_ref, *, blk):
    qi = pl.program_id(1)
    hg, _, e = q_ref.shape
    lat = o_ref.shape[-1]
    rows = hg * blk
    q = q_ref[...].reshape(rows, e)
    _softmax_init(m_ref, l_ref, acc_ref)
    def keys(j):
        return kv_ref[pl.ds(pl.multiple_of(j * blk, blk), blk), :]

    def body(j, s):
        s_next = _scores(q, keys(j + 1))
        _softmax_step(s, keys(j)[:, :lat], m_ref, l_ref, acc_ref)
        return s_next

    s = lax.fori_loop(0, qi, body, _scores(q, keys(0)))
    q_off = lax.broadcasted_iota(jnp.int32, (rows, blk), 0) % blk
    k_off = lax.broadcasted_iota(jnp.int32, (rows, blk), 1)
    s = jnp.where(k_off <= q_off, s, -jnp.inf)
    _softmax_step(s, keys(qi)[:, :lat], m_ref, l_ref, acc_ref)
    out = acc_ref[...] / l_ref[...][:, 0:1]
    o_ref[...] = out.reshape(hg, blk, lat).astype(o_ref.dtype)


def _attn_prompt(q, kv, *, batch, seq, lat):
    h, _, e = q.shape
    blk = _blk(seq, 256)
    hg = _blk(h, 4)
    nq = seq // blk
    return pl.pallas_call(
        functools.partial(_attn_prompt_kernel, blk=blk),
        grid=(batch, nq, h // hg),
        in_specs=[pl.BlockSpec((hg, blk, e), lambda b, i, g: (g, b * nq + i, 0)),
                  pl.BlockSpec((seq, e), lambda b, i, g: (b, 0))],
        out_specs=pl.BlockSpec((hg, blk, lat), lambda b, i, g: (g, b * nq + i, 0)),
        out_shape=jax.ShapeDtypeStruct((h, batch * seq, lat), BF16),
        scratch_shapes=[pltpu.VMEM((hg * blk, LANES), F32), pltpu.VMEM((hg * blk, LANES), F32),
                        pltpu.VMEM((hg * blk, lat), F32)],
        compiler_params=_params("arbitrary", "arbitrary", "arbitrary"),
        name="attn_prompt",
    )(q, kv)


def _attn_paged_kernel(pt_ref, q_ref, new_ref, *rest, n_par, n_new):
    del pt_ref
    cache_refs = rest[:n_par]
    o_ref, m_ref, l_ref, acc_ref = rest[n_par:]
    g = pl.program_id(1)
    h, _, e = q_ref.shape
    lat = o_ref.shape[-1]
    rows = h * n_new
    q = q_ref[...].reshape(rows, e).astype(BF16)

    @pl.when(g == 0)
    def _():
        _softmax_init(m_ref, l_ref, acc_ref)

    size = _blk(n_par, PAGES_PER_SOFTMAX_STEP)
    groups = [cache_refs[i:i + size] for i in range(0, n_par, size)]
    k_ts = [jnp.concatenate([c[0].astype(BF16) for c in grp], axis=1) for grp in groups]
    ss = [_dot(q, k_t) for k_t in k_ts]
    for s, k_t in zip(ss, k_ts):
        _softmax_step(s, k_t[:lat, :], m_ref, l_ref, acc_ref, v_transposed=True)

    @pl.when(g == pl.num_programs(1) - 1)
    def _():
        kn = new_ref[0]
        q_pos = lax.broadcasted_iota(jnp.int32, (rows, kn.shape[0]), 0) % n_new
        k_pos = lax.broadcasted_iota(jnp.int32, (rows, kn.shape[0]), 1)
        s = jnp.where(k_pos <= q_pos, _scores(q, kn), -jnp.inf)
        _softmax_step(s, kn[:, :lat], m_ref, l_ref, acc_ref)
        out = acc_ref[...] / l_ref[...][:, 0:1]
        o_ref[...] = out.reshape(h, n_new, lat)


def _attn_paged(q, kv_new_pad, cache_t, page_table, *, lat):
    h, t, e = q.shape
    b, n_pages = page_table.shape
    n_new = t // b
    page = cache_t.shape[2]
    n_par = _blk(n_pages, PAGES_PER_GRID_STEP)

    def cache_spec(p):
        return pl.BlockSpec((1, e, page), lambda i, g, pt: (pt[i * n_pages + g * n_par + p], 0, 0))

    grid_spec = pltpu.PrefetchScalarGridSpec(
        num_scalar_prefetch=1,
        grid=(b, n_pages // n_par),
        in_specs=[pl.BlockSpec((h, n_new, e), lambda i, g, pt: (0, i, 0)),
                  pl.BlockSpec((1, kv_new_pad.shape[1], e), lambda i, g, pt: (i, 0, 0))]
                 + [cache_spec(p) for p in range(n_par)],
        out_specs=pl.BlockSpec((h, n_new, lat), lambda i, g, pt: (0, i, 0)),
        scratch_shapes=[pltpu.VMEM((h * n_new, LANES), F32), pltpu.VMEM((h * n_new, LANES), F32),
                        pltpu.VMEM((h * n_new, lat), F32)],
    )
    return pl.pallas_call(
        functools.partial(_attn_paged_kernel, n_par=n_par, n_new=n_new),
        grid_spec=grid_spec,
        out_shape=jax.ShapeDtypeStruct((h, t, lat), F32),
        compiler_params=_params("arbitrary", "arbitrary"),
        name="attn_paged",
    )(page_table.reshape(-1), q, kv_new_pad, *([cache_t] * n_par))


def _o_heads_kernel(ol_ref, w_ref, o_ref):
    o_ref[...] = _dot(ol_ref[0].astype(BF16), w_ref[0]).astype(o_ref.dtype)


def _o_heads(o_lat, w_uv_t):
    h, m, lat = o_lat.shape
    v = w_uv_t.shape[2]
    bm = _blk(m, 1024)
    return pl.pallas_call(
        _o_heads_kernel,
        grid=(m // bm, h),
        in_specs=[pl.BlockSpec((1, bm, lat), lambda i, j: (j, i, 0)),
                  pl.BlockSpec((1, lat, v), lambda i, j: (j, 0, 0))],
        out_specs=pl.BlockSpec((bm, v), lambda i, j: (i, j)),
        out_shape=jax.ShapeDtypeStruct((m, h * v), BF16),
        compiler_params=_params("arbitrary", "arbitrary"),
        name="o_heads",
    )(o_lat, w_uv_t)


def _router_kernel(x_ref, w_ref, o_ref, *, n_experts):
    logits = jnp.dot(x_ref[...], w_ref[...], precision=lax.Precision.HIGHEST, preferred_element_type=F32)
    lane = lax.broadcasted_iota(jnp.int32, logits.shape, 1)
    logits = jnp.where(lane < n_experts, logits, -jnp.inf)
    v1 = jnp.max(logits, axis=-1, keepdims=True)
    i1 = jnp.min(jnp.where(logits == v1, lane, LANES), axis=-1, keepdims=True)
    rest = jnp.where(lane == i1, -jnp.inf, logits)
    v2 = jnp.max(rest, axis=-1, keepdims=True)
    i2 = jnp.min(jnp.where(rest == v2, lane, LANES), axis=-1, keepdims=True)
    e2 = jnp.exp(v2 - v1)
    denom = 1.0 + e2
    out = jnp.where(lane == 0, i1.astype(F32),
                    jnp.where(lane == 1, i2.astype(F32),
                              jnp.where(lane == 2, 1.0 / denom, e2 / denom)))
    o_ref[...] = out


def _router(x, router_w):
    m, d = x.shape
    n_experts = router_w.shape[1]
    w = jnp.pad(router_w, ((0, 0), (0, LANES - n_experts)))
    bm = _blk(m, 512)
    meta = pl.pallas_call(
        functools.partial(_router_kernel, n_experts=n_experts),
        grid=(m // bm,),
        in_specs=[pl.BlockSpec((bm, d), lambda i: (i, 0)), pl.BlockSpec((d, LANES), lambda i: (0, 0))],
        out_specs=pl.BlockSpec((bm, LANES), lambda i: (i, 0)),
        out_shape=jax.ShapeDtypeStruct((m, LANES), F32),
        compiler_params=_params("arbitrary"),
        name="router",
    )(x, w)
    return meta[:, :TOP_K].astype(jnp.int32), meta[:, TOP_K:2 * TOP_K]


def _moe_glu_kernel(te_ref, ts_ref, tv_ref, x_ref, wg_ref, wu_ref, o_ref):
    del te_ref, ts_ref
    valid = tv_ref[pl.program_id(1)] > 0

    @pl.when(valid)
    def _():
        x = x_ref[...]
        bn = o_ref.shape[1]
        sub = _blk(bn, MOE_GLU_SUBCOLS)
        for c in range(0, bn, sub):
            g = _dot(x, wg_ref[0, :, c:c + sub].astype(BF16))
            u = _dot(x, wu_ref[0, :, c:c + sub].astype(BF16))
            o_ref[:, c:c + sub] = (jax.nn.silu(g) * u).astype(o_ref.dtype)

    @pl.when(jnp.logical_not(valid))
    def _():
        o_ref[...] = jnp.zeros(o_ref.shape, o_ref.dtype)


def _moe_glu(x_sorted, w_gate, w_up, tile_expert, tile_src, tile_valid):
    p, d = x_sorted.shape
    n = w_gate.shape[2]
    bm, bn = MOE_TILE, _blk(n, 512)
    w_spec = pl.BlockSpec((1, d, bn), lambda j, t, te, ts, tv: (te[t], 0, j))
    grid_spec = pltpu.PrefetchScalarGridSpec(
        num_scalar_prefetch=3,
        grid=(n // bn, p // bm),
        in_specs=[pl.BlockSpec((bm, d), lambda j, t, te, ts, tv: (ts[t], 0)), w_spec, w_spec],
        out_specs=pl.BlockSpec((bm, bn), lambda j, t, te, ts, tv: (t, j)),
    )
    return pl.pallas_call(
        _moe_glu_kernel,
        grid_spec=grid_spec,
        out_shape=jax.ShapeDtypeStruct((p, n), BF16),
        compiler_params=_params("arbitrary", "arbitrary"),
        name="moe_glu",
    )(tile_expert, tile_src, tile_valid, x_sorted, w_gate, w_up)


def _moe_down_kernel(ge_ref, gs_ref, gn_ref, h_ref, w_ref, gate_ref, o_ref, acc_ref, *, nk):
    del ge_ref, gs_ref
    halves = gn_ref[pl.program_id(0)]
    k = pl.program_id(2)
    bn = acc_ref.shape[1]
    sub = _blk(bn, MOE_DOWN_SUBCOLS)

    @pl.when(jnp.logical_and(halves > 0, k == 0))
    def _():
        acc_ref[...] = jnp.zeros(acc_ref.shape, F32)

    @pl.when(halves == 2)
    def _():
        h = h_ref[...]
        for c in range(0, bn, sub):
            acc_ref[:, c:c + sub] += _dot(h, w_ref[0, :, c:c + sub].astype(BF16))

    @pl.when(halves == 1)
    def _():
        h = h_ref[:MOE_TILE, :]
        for c in range(0, bn, sub):
            acc_ref[:MOE_TILE, c:c + sub] += _dot(h, w_ref[0, :, c:c + sub].astype(BF16))

    @pl.when(jnp.logical_and(halves > 0, k == nk - 1))
    def _():
        o_ref[...] = acc_ref[...] * gate_ref[...]

    @pl.when(halves == 0)
    def _():
        o_ref[...] = jnp.zeros(o_ref.shape, o_ref.dtype)


def _moe_down(h_sorted, w_down, gate_sorted, group_expert, group_src, group_halves):
    p, f = h_sorted.shape
    d = w_down.shape[2]
    bm, bn, bk = MOE_GROUP, _blk(d, 2048), _blk(f, 1024)
    nk = f // bk
    kk = lambda g, k, gn: jnp.where(gn[g] > 0, k, nk - 1)
    grid_spec = pltpu.PrefetchScalarGridSpec(
        num_scalar_prefetch=3,
        grid=(p // bm, d // bn, nk),
        in_specs=[pl.BlockSpec((bm, bk), lambda g, j, k, ge, gs, gn: (gs[g], kk(g, k, gn))),
                  pl.BlockSpec((1, bk, bn), lambda g, j, k, ge, gs, gn: (ge[g], kk(g, k, gn), j)),
                  pl.BlockSpec((bm, 1), lambda g, j, k, ge, gs, gn: (g, 0))],
        out_specs=pl.BlockSpec((bm, bn), lambda g, j, k, ge, gs, gn: (g, j)),
        scratch_shapes=[pltpu.VMEM((bm, bn), F32)],
    )
    return pl.pallas_call(
        functools.partial(_moe_down_kernel, nk=nk),
        grid_spec=grid_spec,
        out_shape=jax.ShapeDtypeStruct((p, d), F32),
        compiler_params=_params("arbitrary", "arbitrary", "arbitrary"),
        name="moe_down",
    )(group_expert, group_src, group_halves, h_sorted, w_down, gate_sorted)


def _route(idx, gate, n_experts):
    t = idx.shape[0]
    n_pairs = t * TOP_K
    n_groups = (n_pairs + n_experts * (MOE_GROUP - 1)) // MOE_GROUP
    tiles_per_group = MOE_GROUP // MOE_TILE
    n_tiles = n_groups * tiles_per_group
    e_flat = idx.reshape(-1)
    onehot = (e_flat[:, None] == jnp.arange(n_experts, dtype=jnp.int32)[None, :]).astype(jnp.int32)
    csum = jnp.cumsum(onehot, axis=0)
    counts = csum[-1]
    rank = jnp.sum((csum - onehot) * onehot, axis=1)
    padded = (counts + MOE_GROUP - 1) // MOE_GROUP * MOE_GROUP
    pend = jnp.cumsum(padded)
    pstart = pend - padded
    real_end = pstart + counts
    dest = pstart[e_flat] + rank
    token = jnp.arange(n_pairs, dtype=jnp.int32) // TOP_K
    src = jnp.zeros((n_groups * MOE_GROUP,), jnp.int32).at[dest].set(token)
    gate_sorted = jnp.zeros((n_groups * MOE_GROUP,), F32).at[dest].set(gate.reshape(-1))

    tile_ids = jnp.arange(n_tiles, dtype=jnp.int32)
    tile_row = tile_ids * MOE_TILE
    tile_e = jnp.minimum(jnp.sum((tile_row[:, None] >= pend[None, :]).astype(jnp.int32), axis=1), n_experts - 1)
    tile_valid = jnp.logical_and(tile_row < pend[-1], tile_row < real_end[tile_e])
    tile_src = lax.cummax(jnp.where(tile_valid, tile_ids, 0))
    tile_expert = tile_e[tile_src]
    per_group = lambda a: a.reshape(n_groups, tiles_per_group)
    group_halves = jnp.sum(per_group(tile_valid.astype(jnp.int32)), axis=1)
    group_ids = jnp.arange(n_groups, dtype=jnp.int32)
    group_src = lax.cummax(jnp.where(group_halves > 0, group_ids, 0))
    group_expert = per_group(tile_e)[:, 0][group_src]
    tiles = (tile_expert.astype(jnp.int32), tile_src.astype(jnp.int32), tile_valid.astype(jnp.int32))
    groups = (group_expert.astype(jnp.int32), group_src.astype(jnp.int32), group_halves.astype(jnp.int32))
    return src, gate_sorted[:, None], dest.reshape(t, TOP_K), tiles, groups


def _rope_tables(pos, rope):
    inv = ROPE_THETA ** (-jnp.arange(0, rope, 2, dtype=F32) / rope)
    ang = pos.astype(F32)[:, None] * inv[None, :]
    cos, sin = jnp.cos(ang), jnp.sin(ang)
    return jnp.concatenate([cos, cos], axis=-1), jnp.concatenate([-sin, sin], axis=-1)


def kernel(x_prompt, x_sample, state_conv, cache_mla, page_table, ln_mix_g, ln_mix_b, ln_ffn_g, ln_ffn_b, conv_w_in, conv_w, conv_w_out, kv_w_down, kv_norm_g, w_uk, w_uv, q_w_down, q_norm_g, q_w_up, attn_w_out, ffn_w_gate, ffn_w_up, ffn_w_down, router_w, expert_w_gate, expert_w_up, expert_w_down):
    batch, seq, d = x_prompt.shape
    dec_batch, dec_seq, _ = x_sample.shape
    n_conv_layers, conv_w_len, dc = conv_w.shape
    lat, n_heads, nope = w_uk.shape
    v_head = w_uv.shape[2]
    rope = q_w_up.shape[3] - nope
    n_experts = router_w.shape[2]
    page = cache_mla.shape[1]
    past_len = page_table.shape[1] * page
    assert n_conv_layers == 1 and conv_w_len == 3 and dec_seq >= conv_w_len - 1 and seq >= conv_w_len - 1
    assert q_w_down.shape[0] == 1 and router_w.shape[0] == 1 and ffn_w_gate.shape[0] == 1
    t_p, t_s = batch * seq, dec_batch * dec_seq
    scale = float(nope + rope) ** -0.5

    x = jnp.concatenate([x_prompt.reshape(t_p, d), x_sample.reshape(t_s, d)], axis=0)
    pos = jnp.concatenate([jnp.tile(jnp.arange(seq), batch), jnp.tile(past_len + jnp.arange(dec_seq), dec_batch)])
    cos2, sin2 = _rope_tables(pos, rope)

    gb, u = _conv_in(x.astype(BF16), conv_w_in[0].astype(BF16))
    prev = state_conv[0].astype(F32)
    pad_rows = lambda a: jnp.pad(a, ((0, 0), (0, dec_seq - a.shape[1]), (0, 0))).reshape(t_s, dc)
    hist = (pad_rows(prev[:, 1:2]), pad_rows(prev))
    g_p = _conv_gate(u, gb, conv_w[0], None, seq_len=seq, rows=seq, row0=0, nrows=t_p)
    g_s = _conv_gate(u, gb, conv_w[0], hist, seq_len=dec_seq, rows=_blk(t_s, 1024), row0=t_p, nrows=t_s)
    mix = _matmul(jnp.concatenate([g_p, g_s], axis=0), conv_w_out[0].astype(BF16),
                  bm=1024, bn=1024, bk=d, out_dtype=F32)
    conv_prompt = u[:t_p].reshape(batch, seq, dc)[:, seq - 2:][None]
    conv_sample = u[t_p:].reshape(dec_batch, dec_seq, dc)[:, dec_seq - 2:][None]
    x, xb = _ln_residual(x, (mix,), ln_mix_g[0], ln_mix_b[0])

    hid = _glu(xb, ffn_w_gate[0].astype(BF16), ffn_w_up[0].astype(BF16))
    ffn = _matmul(hid, ffn_w_down[0].astype(BF16), bm=1024, bn=1024, bk=2048, out_dtype=F32)
    x, xb = _ln_residual(x, (ffn,), ln_ffn_g[0], ln_ffn_b[0])

    rows, rows_b = _kv_rows(xb, kv_w_down.astype(BF16), kv_norm_g, cos2, sin2)
    e = rows.shape[1]

    cq = _mm_rms(xb, q_w_down[0].astype(BF16), q_norm_g[0])
    w_nope = q_w_up[0, :, :, :nope].transpose(1, 0, 2).astype(BF16)
    w_rope = q_w_up[0, :, :, nope:].transpose(1, 0, 2).astype(BF16)
    w_uk_t = w_uk.transpose(1, 2, 0).astype(BF16)
    w_uv_t = w_uv.transpose(1, 0, 2).astype(BF16)
    heads = functools.partial(_q_heads, cq, w_nope, w_rope, w_uk_t, cos2, sin2, scale=scale)
    q_p = heads(row0=0, nrows=t_p, out_dtype=BF16)
    q_s = heads(row0=t_p, nrows=t_s, out_dtype=F32)
    o_p = _attn_prompt(q_p, rows_b, batch=batch, seq=seq, lat=lat)
    kv_new = jnp.pad(rows_b[t_p:].reshape(dec_batch, dec_seq, e), ((0, 0), (0, LANES - dec_seq), (0, 0)))
    o_s = _attn_paged(q_s, kv_new, jnp.swapaxes(cache_mla, 1, 2), page_table, lat=lat)
    o = jnp.concatenate([_o_heads(o_p, w_uv_t), _o_heads(o_s, w_uv_t)], axis=0)
    mix = _matmul(o, attn_w_out[0].astype(BF16), bm=1024, bn=1024, bk=n_heads * v_head, out_dtype=F32)
    x, xb = _ln_residual(x, (mix,), ln_mix_g[1], ln_mix_b[1])

    idx, gate = _router(x, router_w[0])
    src, gate_sorted, dest, tiles, groups = _route(idx, gate, n_experts)
    hid = _moe_glu(xb[src], expert_w_gate[0], expert_w_up[0], *tiles)
    y_sorted = _moe_down(hid, expert_w_down[0], gate_sorted, *groups)
    x, _ = _ln_residual(x, (y_sorted[dest[:, 0]], y_sorted[dest[:, 1]]), ln_ffn_g[1], ln_ffn_b[1])

    return (x[:t_p].reshape(batch, seq, d), x[t_p:].reshape(dec_batch, dec_seq, d),
            conv_prompt, conv_sample,
            rows[:t_p].reshape(batch, seq, e), rows[t_p:].reshape(dec_batch, dec_seq, e))
```
